```python
import math
import jax, jax.numpy as jnp
from jax import lax
import numpy as np

D_MODEL = 1024
BATCH = 16
SEQ = 2048
DEPTH = 1
DEC_BATCH = 8
DEC_SEQ = 2048
PAST_LEN = 128

N_META = 16
GRID_W = 64
CONV_WIDTH = 512
CONV_GROUPS = 8
CONV_KERNEL = 3
N_HEADS = 8
N_KV_HEADS = 2
Q_PER_KV = N_HEADS // N_KV_HEADS
HEAD_DIM = 64
AXIS_DIM = HEAD_DIM // 2
ROPE_THETA = 10000.0
Q_BLOCK = 128
ATTN_WIDTH = N_HEADS * HEAD_DIM
KV_WIDTH = N_KV_HEADS * HEAD_DIM
MIX_WIDTH = CONV_WIDTH + ATTN_WIDTH
IN_WIDTH = 3 * CONV_WIDTH + ATTN_WIDTH + 2 * KV_WIDTH
IN_SPLITS = [CONV_WIDTH, 2 * CONV_WIDTH, 3 * CONV_WIDTH,
             3 * CONV_WIDTH + ATTN_WIDTH, 3 * CONV_WIDTH + ATTN_WIDTH + KV_WIDTH]
PEER_HEADS = 8
PEER_NKEYS = 128
PEER_EXPERTS = PEER_NKEYS * PEER_NKEYS
PEER_QDIM = 256
PEER_HALF = PEER_QDIM // 2
PEER_TOPK = 16
PEER_CHUNK = 256
EPS = 1e-6

kernel_name = 'hymba_conv_gqa_peer_encoder'


def rmsnorm(x, g):
    x32 = x.astype(jnp.float32)
    y = x32 * lax.rsqrt(jnp.mean(x32 * x32, axis=-1, keepdims=True) + EPS)
    return (y * g.astype(jnp.float32)).astype(x.dtype)


def group_rmsnorm(x, g, n_groups):
    shp = x.shape
    x32 = x.astype(jnp.float32).reshape(shp[:-1] + (n_groups, shp[-1] // n_groups))
    y = x32 * lax.rsqrt(jnp.mean(x32 * x32, axis=-1, keepdims=True) + EPS)
    return (y.reshape(shp) * g.astype(jnp.float32)).astype(x.dtype)


def axial_rope_tables(n_tokens):
    rows = n_tokens // GRID_W
    row_ids = jnp.repeat(jnp.arange(rows, dtype=jnp.int32), GRID_W)
    col_ids = jnp.tile(jnp.arange(GRID_W, dtype=jnp.int32), rows)
    meta = jnp.zeros((N_META,), jnp.int32)
    row = jnp.concatenate([meta, row_ids]).astype(jnp.float32)
    col = jnp.concatenate([meta, col_ids]).astype(jnp.float32)
    freqs = ROPE_THETA ** (-jnp.arange(0, AXIS_DIM, 2, dtype=jnp.float32) / AXIS_DIM)
    ang_r = row[:, None] * freqs[None, :]
    ang_c = col[:, None] * freqs[None, :]
    return (jnp.cos(ang_r), jnp.sin(ang_r), jnp.cos(ang_c), jnp.sin(ang_c))


def _rotate(x, cos, sin):
    half = x.shape[-1] // 2
    x1, x2 = x[..., :half], x[..., half:]
    cos = cos[None, :, None, :]
    sin = sin[None, :, None, :]
    return jnp.concatenate([x1 * cos - x2 * sin, x2 * cos + x1 * sin], axis=-1)


def apply_axial_rope(x, tables):
    cos_r, sin_r, cos_c, sin_c = tables
    x32 = x.astype(jnp.float32)
    out = jnp.concatenate([_rotate(x32[..., :AXIS_DIM], cos_r, sin_r),
                           _rotate(x32[..., AXIS_DIM:], cos_c, sin_c)], axis=-1)
    return out.astype(x.dtype)


def attend_block(qb, k, v):
    s = jnp.einsum('bqgrd,bkgd->bgrqk', qb, k, preferred_element_type=jnp.float32) * (HEAD_DIM ** -0.5)
    p = jax.nn.softmax(s, axis=-1).astype(v.dtype)
    return jnp.einsum('bgrqk,bkgd->bqgrd', p, v)


def gqa_attention(q, k, v):
    Bx, L = q.shape[0], q.shape[1]
    n = L - N_META
    out_meta = attend_block(q[:, :N_META], k, v)
    q_real = q[:, N_META:].reshape(Bx, n // Q_BLOCK, Q_BLOCK, N_KV_HEADS, Q_PER_KV, HEAD_DIM)
    q_real = jnp.moveaxis(q_real, 1, 0)
    out_real = lax.map(lambda qb: attend_block(qb, k, v), q_real)
    out_real = jnp.moveaxis(out_real, 0, 1).reshape(Bx, n, N_KV_HEADS, Q_PER_KV, HEAD_DIM)
    return jnp.concatenate([out_meta, out_real], axis=1)


def depthwise_conv3(u, w):
    up = jnp.pad(u, ((0, 0), (1, 1), (0, 0)))
    return up[:, :-2] * w[0] + up[:, 1:-1] * w[1] + up[:, 2:] * w[2]


def peer_ffn(x, wq, subkeys, u_tab, v_tab):
    Bx, L, D = x.shape
    T = Bx * L
    xf = x.reshape(T, D)
    q = (xf @ wq).reshape(T, PEER_HEADS, 2, PEER_HALF)
    s = jnp.einsum('thpc,hpnc->thpn', q, subkeys, preferred_element_type=jnp.float32)
    sv, si = lax.top_k(s, PEER_TOPK)
    cand = (sv[..., 0, :, None] + sv[..., 1, None, :]).reshape(T, PEER_HEADS, PEER_TOPK * PEER_TOPK)
    cand_idx = (si[..., 0, :, None] * PEER_NKEYS + si[..., 1, None, :]).reshape(T, PEER_HEADS, PEER_TOPK * PEER_TOPK)
    cv, cp = lax.top_k(cand, PEER_TOPK)
    eidx = jnp.take_along_axis(cand_idx, cp, axis=-1)
    gates = jax.nn.softmax(cv, axis=-1).astype(x.dtype)
    pad = (-T) % PEER_CHUNK
    nc = (T + pad) // PEER_CHUNK
    xp = jnp.pad(xf, ((0, pad), (0, 0))).reshape(nc, PEER_CHUNK, D)
    ip = jnp.pad(eidx, ((0, pad), (0, 0), (0, 0))).reshape(nc, PEER_CHUNK, PEER_HEADS, PEER_TOPK)
    gp = jnp.pad(gates, ((0, pad), (0, 0), (0, 0))).reshape(nc, PEER_CHUNK, PEER_HEADS, PEER_TOPK)

    def chunk(args):
        xc, ic, gc = args
        uc = jnp.take(u_tab, ic, axis=0)
        a = jax.nn.gelu(jnp.einsum('thkd,td->thk', uc, xc), approximate=False) * gc
        vc = jnp.take(v_tab, ic, axis=0)
        return jnp.einsum('thk,thkd->td', a, vc)

    out = lax.map(chunk, (xp, ip, gp))
    return out.reshape(nc * PEER_CHUNK, D)[:T].reshape(Bx, L, D)


def layer(h, tables, norm_mix_g, w_in, conv_w, q_norm_g, k_norm_g, conv_out_g, attn_out_g,
          w_out, norm_ffn_g, peer_wq, peer_subkeys, peer_u, peer_v):
    Bx, L, _ = h.shape
    xn = rmsnorm(h, norm_mix_g)
    z = xn @ w_in
    gate_b, gate_c, hc, q, k, v = jnp.split(z, IN_SPLITS, axis=-1)
    y_conv = gate_b * depthwise_conv3(gate_c * hc, conv_w)
    y_conv = group_rmsnorm(y_conv, conv_out_g, CONV_GROUPS)
    q = rmsnorm(q.reshape(Bx, L, N_HEADS, HEAD_DIM), q_norm_g)
    k = rmsnorm(k.reshape(Bx, L, N_KV_HEADS, HEAD_DIM), k_norm_g)
    q = apply_axial_rope(q, tables).reshape(Bx, L, N_KV_HEADS, Q_PER_KV, HEAD_DIM)
    k = apply_axial_rope(k, tables)
    v = v.reshape(Bx, L, N_KV_HEADS, HEAD_DIM)
    y_attn = gqa_attention(q, k, v).reshape(Bx, L, ATTN_WIDTH)
    y_attn = group_rmsnorm(y_attn, attn_out_g, N_HEADS)
    h = h + jnp.concatenate([y_conv, y_attn], axis=-1) @ w_out
    h = h + peer_ffn(rmsnorm(h, norm_ffn_g), peer_wq, peer_subkeys, peer_u, peer_v)
    return h


def encode(x, meta_tokens, norm_mix_g, w_in, conv_w, q_norm_g, k_norm_g, conv_out_g, attn_out_g,
           w_out, norm_ffn_g, peer_wq, peer_subkeys, peer_u, peer_v):
    Bx, n, _ = x.shape
    meta = jnp.broadcast_to(meta_tokens.astype(x.dtype)[None], (Bx, N_META, D_MODEL))
    h = jnp.concatenate([meta, x], axis=1)
    tables = axial_rope_tables(n)
    for l in range(DEPTH):
        h = layer(h, tables, norm_mix_g[l], w_in[l], conv_w[l], q_norm_g[l], k_norm_g[l],
                  conv_out_g[l], attn_out_g[l], w_out[l], norm_ffn_g[l], peer_wq[l],
                  peer_subkeys[l], peer_u[l], peer_v[l])
    return h[:, N_META:]


def setup_inputs(seed: int = 0) -> dict:
    key = jax.random.key(seed)
    ks = jax.random.split(key, 16)
    nrm = jax.random.normal
    f32 = jnp.float32
    return {
        'x_prompt': nrm(ks[0], (BATCH, SEQ, D_MODEL), f32),
        'x_sample': nrm(ks[1], (DEC_BATCH, DEC_SEQ, D_MODEL), f32),
        'meta_tokens': nrm(ks[2], (N_META, D_MODEL), f32),
        'norm_mix_g': 1.0 + 0.05 * nrm(ks[3], (DEPTH, D_MODEL), f32),
        'w_in': nrm(ks[4], (DEPTH, D_MODEL, IN_WIDTH), f32) * D_MODEL ** -0.5,
        'conv_w': nrm(ks[5], (DEPTH, CONV_KERNEL, CONV_WIDTH), f32) * CONV_KERNEL ** -0.5,
        'q_norm_g': 1.0 + 0.05 * nrm(ks[6], (DEPTH, HEAD_DIM), f32),
        'k_norm_g': 1.0 + 0.05 * nrm(ks[7], (DEPTH, HEAD_DIM), f32),
        'conv_out_g': 1.0 + 0.05 * nrm(ks[8], (DEPTH, CONV_WIDTH), f32),
        'attn_out_g': 1.0 + 0.05 * nrm(ks[9], (DEPTH, ATTN_WIDTH), f32),
        'w_out': nrm(ks[10], (DEPTH, MIX_WIDTH, D_MODEL), f32) * MIX_WIDTH ** -0.5,
        'norm_ffn_g': 1.0 + 0.05 * nrm(ks[11], (DEPTH, D_MODEL), f32),
        'peer_wq': nrm(ks[12], (DEPTH, D_MODEL, PEER_HEADS * PEER_QDIM), f32) * D_MODEL ** -0.5,
        'peer_subkeys': nrm(ks[13], (DEPTH, PEER_HEADS, 2, PEER_NKEYS, PEER_HALF), f32) * PEER_HALF ** -0.5,
        'peer_u': nrm(ks[14], (DEPTH, PEER_EXPERTS, D_MODEL), f32) * D_MODEL ** -0.5,
        'peer_v': nrm(ks[15], (DEPTH, PEER_EXPERTS, D_MODEL), f32) * 0.25,
    }


def reference(x_prompt, x_sample, meta_tokens, norm_mix_g, w_in, conv_w, q_norm_g, k_norm_g,
              conv_out_g, attn_out_g, w_out, norm_ffn_g, peer_wq, peer_subkeys, peer_u, peer_v):
    y_prompt = encode(x_prompt, meta_tokens, norm_mix_g, w_in, conv_w, q_norm_g, k_norm_g,
                      conv_out_g, attn_out_g, w_out, norm_ffn_g, peer_wq, peer_subkeys, peer_u, peer_v)
    y_sample = encode(x_sample, meta_tokens, norm_mix_g, w_in, conv_w, q_norm_g, k_norm_g,
                      conv_out_g, attn_out_g, w_out, norm_ffn_g, peer_wq, peer_subkeys, peer_u, peer_v)
    return (y_prompt, y_sample)
```

```python
import functools

import jax
import jax.numpy as jnp
from jax import lax
from jax.experimental import pallas as pl
from jax.experimental.pallas import tpu as pltpu

D_MODEL = 1024
N_META = 16
GRID_W = 64
CONV_WIDTH = 512
CONV_GROUPS = 8
N_HEADS = 8
N_KV_HEADS = 2
Q_PER_KV = N_HEADS // N_KV_HEADS
HEAD_DIM = 64
AXIS_DIM = HEAD_DIM // 2
ROPE_THETA = 10000.0
ATTN_WIDTH = N_HEADS * HEAD_DIM
KV_WIDTH = N_KV_HEADS * HEAD_DIM
IN_WIDTH = 3 * CONV_WIDTH + ATTN_WIDTH + 2 * KV_WIDTH
PEER_HEADS = 8
PEER_NKEYS = 128
PEER_EXPERTS = PEER_NKEYS * PEER_NKEYS
PEER_QDIM = 256
PEER_HALF = PEER_QDIM // 2
PEER_TOPK = 16
EPS = 1e-6

LANES = 128
SUBLANES = 8
META_PAD = LANES
NEG_BIG = -1e30
NOT_RANKED = 99.0
VMEM_LIMIT = 56 * 1024 * 1024

F32 = jnp.float32
BF16 = jnp.bfloat16
HI = lax.Precision.HIGHEST


def _rms_rows(x, g):
    return x * lax.rsqrt(jnp.mean(x * x, axis=-1, keepdims=True) + EPS) * g


def _group_rms(x, gmat, g):
    ms = jnp.dot(x * x, gmat, precision=HI, preferred_element_type=F32)
    return x * lax.rsqrt(ms + EPS) * g


def _rope(x, cos, sin):
    lane = lax.broadcasted_iota(jnp.int32, x.shape, 1)
    swapped = jnp.where(lane % (2 * (AXIS_DIM // 2)) < AXIS_DIM // 2,
                        pltpu.roll(x, LANES - AXIS_DIM // 2, 1), pltpu.roll(x, AXIS_DIM // 2, 1))
    return x * cos + swapped * sin


def _meta_proj_kernel(m_ref, g_ref, w_ref, kg_ref, gm_ref, u_ref, k_ref, v_ref):
    xn = _rms_rows(m_ref[...], g_ref[...]).astype(BF16)
    z = jnp.dot(xn, w_ref[...], preferred_element_type=F32)
    u_ref[...] = z[:, CONV_WIDTH:2 * CONV_WIDTH] * z[:, 2 * CONV_WIDTH:3 * CONV_WIDTH]
    k0 = 3 * CONV_WIDTH + ATTN_WIDTH
    kn = _group_rms(z[:, k0:k0 + KV_WIDTH], gm_ref[...], kg_ref[...])
    for g in range(N_KV_HEADS):
        k_ref[g] = kn[:, g * HEAD_DIM:(g + 1) * HEAD_DIM].astype(BF16)
        v_ref[g] = z[:, k0 + KV_WIDTH + g * HEAD_DIM:k0 + KV_WIDTH + (g + 1) * HEAD_DIM].astype(BF16)


def _in_proj_kernel(x_ref, g_ref, w_ref, cw_ref, qg_ref, kg_ref, cog_ref, cos_ref, sin_ref, um_ref,
                    gm_ref, yconv_ref, q_ref, k_ref, v_ref, u_scr, gb_scr, *, n, rc):
    c = pl.program_id(1)
    r0 = pl.multiple_of(c * rc, rc)
    halo = SUBLANES

    @pl.when(c == 0)
    def _():
        u_scr[0:halo, :] = jnp.broadcast_to(um_ref[...], (halo, CONV_WIDTH))
        u_scr[halo + n:2 * halo + n, :] = jnp.zeros((halo, CONV_WIDTH), F32)

    xn = _rms_rows(x_ref[0], g_ref[...]).astype(BF16)
    z = jnp.dot(xn, w_ref[...], preferred_element_type=F32)
    gb_scr[pl.ds(r0, rc), :] = z[:, 0:CONV_WIDTH]
    u_scr[pl.ds(halo + r0, rc), :] = z[:, CONV_WIDTH:2 * CONV_WIDTH] * z[:, 2 * CONV_WIDTH:3 * CONV_WIDTH]

    cos = cos_ref[...]
    sin = sin_ref[...]
    q0 = 3 * CONV_WIDTH
    qn = _group_rms(z[:, q0:q0 + ATTN_WIDTH], gm_ref[...], qg_ref[...])
    for j in range(ATTN_WIDTH // LANES):
        qr = _rope(qn[:, j * LANES:(j + 1) * LANES], cos, sin) * (HEAD_DIM ** -0.5)
        q_ref[0, 2 * j] = qr[:, :HEAD_DIM].astype(BF16)
        q_ref[0, 2 * j + 1] = qr[:, HEAD_DIM:].astype(BF16)
    k0 = q0 + ATTN_WIDTH
    kr = _rope(_group_rms(z[:, k0:k0 + KV_WIDTH], gm_ref[0:KV_WIDTH, 0:KV_WIDTH], kg_ref[...]), cos, sin)
    for g in range(N_KV_HEADS):
        k_ref[0, g] = kr[:, g * HEAD_DIM:(g + 1) * HEAD_DIM].astype(BF16)
        v_ref[0, g] = z[:, k0 + KV_WIDTH + g * HEAD_DIM:k0 + KV_WIDTH + (g + 1) * HEAD_DIM].astype(BF16)

    @pl.when(c == pl.num_programs(1) - 1)
    def _():
        for j in range(n // rc):
            s = j * rc
            um = u_scr[halo + s - 1:halo + s - 1 + rc, :]
            uc = u_scr[halo + s:halo + s + rc, :]
            up = u_scr[halo + s + 1:halo + s + 1 + rc, :]
            y = gb_scr[s:s + rc, :] * (um * cw_ref[0:1, :] + uc * cw_ref[1:2, :] + up * cw_ref[2:3, :])
            yconv_ref[0, s:s + rc, :] = _group_rms(y, gm_ref[...], cog_ref[...]).astype(BF16)


def _attn_out_kernel(x_ref, yconv_ref, q_ref, k_ref, v_ref, km_ref, vm_ref, bias_ref, ag_ref, wo_ref,
                     h_ref, *, tq):
    acc = x_ref[0] + jnp.dot(yconv_ref[0], wo_ref[0:CONV_WIDTH, :], preferred_element_type=F32)
    dn = (((1,), (1,)), ((), ()))
    for g in range(N_KV_HEADS):
        qg = q_ref[0, g * Q_PER_KV:(g + 1) * Q_PER_KV].reshape(Q_PER_KV * tq, HEAD_DIM)
        s = lax.dot_general(qg, k_ref[0, g], dn, preferred_element_type=F32)
        sm = lax.dot_general(qg, km_ref[g], dn, preferred_element_type=F32) + bias_ref[...]
        m = jnp.maximum(jnp.max(s, axis=-1, keepdims=True), jnp.max(sm, axis=-1, keepdims=True))
        p = jnp.exp(s - m)
        pm = jnp.exp(sm - m)
        l = jnp.sum(p, axis=-1, keepdims=True) + jnp.sum(pm, axis=-1, keepdims=True)
        o = (jnp.dot(p.astype(BF16), v_ref[0, g], preferred_element_type=F32)
             + jnp.dot(pm.astype(BF16), vm_ref[g], preferred_element_type=F32)) / l
        for r in range(Q_PER_KV):
            head = g * Q_PER_KV + r
            y = _rms_rows(o[r * tq:(r + 1) * tq], ag_ref[head:head + 1, :]).astype(BF16)
            w0 = CONV_WIDTH + head * HEAD_DIM
            acc = acc + jnp.dot(y, wo_ref[w0:w0 + HEAD_DIM, :], preferred_element_type=F32)
    h_ref[0] = acc


def _top16_ranks(s):
    rows = lax.broadcasted_iota(jnp.int32, s.shape, 0)
    r16 = lax.broadcasted_iota(jnp.int32, (PEER_TOPK, LANES), 0)
    rank = jnp.full(s.shape, NOT_RANKED, F32)
    vals = jnp.zeros((PEER_TOPK, LANES), F32)
    for a in range(PEER_TOPK):
        m = jnp.max(s, axis=0, keepdims=True)
        idx = jnp.min(jnp.where(s == m, rows, PEER_NKEYS), axis=0, keepdims=True)
        sel = rows == idx
        rank = jnp.where(sel, float(a), rank)
        s = jnp.where(sel, -jnp.inf, s)
        vals = jnp.where(r16 == a, m, vals)
    return rank, vals


def _select_pairs(v1, v2):
    r8 = lax.broadcasted_iota(jnp.int32, (SUBLANES, LANES), 0)

    def bc(row):
        return jnp.broadcast_to(row, (SUBLANES, LANES))

    cands = [bc(v1[0:1]) + v2[0:8], bc(v1[0:1]) + v2[8:16]]
    poss = [r8, r8 + 8]
    for a in range(1, 8):
        cands.append(jnp.where(r8 < PEER_TOPK // (a + 1), bc(v1[a:a + 1]) + v2[0:8], -jnp.inf))
        poss.append(r8 + PEER_TOPK * a)
    cands.append(v1[8:16] + bc(v2[0:1]))
    poss.append((r8 + 8) * PEER_TOPK)
    cand0 = jnp.concatenate(cands, axis=0)
    pos = jnp.concatenate(poss, axis=0)
    cand = cand0
    sel = jnp.zeros(cand.shape, jnp.bool_)
    for _ in range(PEER_TOPK):
        m = jnp.max(cand, axis=0, keepdims=True)
        first = jnp.min(jnp.where(cand == m, pos, PEER_TOPK * PEER_TOPK), axis=0, keepdims=True)
        hit = pos == first
        sel = jnp.logical_or(sel, hit)
        cand = jnp.where(hit, -jnp.inf, cand)
    self32 = sel.astype(F32)
    z = jnp.sum(jnp.where(sel, jnp.exp(cand0 - cand0[0:1]), 0.0), axis=0, keepdims=True)
    n_lo = jnp.zeros((SUBLANES, LANES), F32)
    n0 = jnp.sum(self32[0:16], axis=0, keepdims=True)
    n_lo = jnp.where(r8 == 0, n0, n_lo)
    for a in range(1, 8):
        na = jnp.sum(self32[8 + 8 * a:16 + 8 * a], axis=0, keepdims=True)
        n_lo = jnp.where(r8 == a, na, n_lo)
    return jnp.concatenate([n_lo, self32[72:80]], axis=0), z


def _peer_kernel(h_ref, g_ref, wqt_ref, sk_ref, u_ref, vt_ref, out_ref,
                 xnt_scr, sc_scr, rk2_scr, e2_scr, n_scr, r_scr, acc_scr, p_scr, *, tt, et):
    e = pl.program_id(1)
    nc = tt // LANES
    n_i = et // PEER_NKEYS

    @pl.when(e == 0)
    def _prologue():
        xn = _rms_rows(h_ref[...], g_ref[...])
        xnt = xn.T.astype(BF16)
        xnt_scr[...] = xnt
        qt = jnp.dot(wqt_ref[...], xnt, preferred_element_type=F32).astype(BF16)
        for hp in range(2 * PEER_HEADS):
            st = jnp.dot(sk_ref[hp], qt[hp * PEER_HALF:(hp + 1) * PEER_HALF, :],
                         preferred_element_type=F32)
            for c in range(nc):
                sc_scr[hp, c] = st[:, c * LANES:(c + 1) * LANES]

        def head_body(hh, carry):
            def chunk_body(c, carry2):
                s1 = sc_scr[2 * hh, c]
                s2 = sc_scr[2 * hh + 1, c]
                rk1, v1 = _top16_ranks(s1)
                rk2, v2 = _top16_ranks(s2)
                counts, z = _select_pairs(v1, v2)
                n_dense = jnp.zeros(s1.shape, F32)
                for a in range(PEER_TOPK):
                    n_dense = jnp.where(rk1 == float(a), counts[a:a + 1], n_dense)
                e1 = jnp.exp(s1 - v1[0:1])
                rk2_scr[hh, c] = rk2
                e2_scr[hh, c] = jnp.exp(s2 - v2[0:1])
                n_scr[hh, c] = n_dense
                r_scr[hh, c] = jnp.where(n_dense > 0.0, e1 / z, 0.0)
                return carry2
            return lax.fori_loop(0, nc, chunk_body, carry)
        lax.fori_loop(0, PEER_HEADS, head_body, 0)
        acc_scr[...] = jnp.zeros(acc_scr.shape, F32)

    a_t = jnp.dot(u_ref[...], xnt_scr[...], preferred_element_type=F32)
    hid = 0.5 * a_t * (1.0 + lax.erf(a_t * (2.0 ** -0.5)))
    for ii in range(n_i):
        i = e * n_i + ii
        for c in range(nc):
            w = jnp.zeros((PEER_NKEYS, LANES), F32)
            for hh in range(PEER_HEADS):
                n_row = n_scr[hh, c, pl.ds(i, 1), :]
                r_row = r_scr[hh, c, pl.ds(i, 1), :]
                w = w + jnp.where(rk2_scr[hh, c] < n_row, e2_scr[hh, c], 0.0) * r_row
            blk = hid[ii * PEER_NKEYS:(ii + 1) * PEER_NKEYS, c * LANES:(c + 1) * LANES] * w
            p_scr[ii * PEER_NKEYS:(ii + 1) * PEER_NKEYS, c * LANES:(c + 1) * LANES] = blk.astype(BF16)
    acc_scr[...] += jnp.dot(vt_ref[...], p_scr[...], preferred_element_type=F32)

    @pl.when(e == pl.num_programs(1) - 1)
    def _epilogue():
        out_ref[...] = h_ref[...] + acc_scr[...].T


def _rope_tables(n):
    rows = n // GRID_W
    row = jnp.repeat(jnp.arange(rows, dtype=jnp.int32), GRID_W).astype(F32)
    col = jnp.tile(jnp.arange(GRID_W, dtype=jnp.int32), rows).astype(F32)
    freqs = ROPE_THETA ** (-jnp.arange(0, AXIS_DIM, 2, dtype=F32) / AXIS_DIM)
    ang_r = row[:, None] * freqs[None, :]
    ang_c = col[:, None] * freqs[None, :]
    cos_h = jnp.concatenate([jnp.cos(ang_r)] * 2 + [jnp.cos(ang_c)] * 2, axis=-1)
    sin_h = jnp.concatenate([-jnp.sin(ang_r), jnp.sin(ang_r), -jnp.sin(ang_c), jnp.sin(ang_c)], axis=-1)
    return jnp.tile(cos_h, (1, LANES // HEAD_DIM)), jnp.tile(sin_h, (1, LANES // HEAD_DIM))


def _tile_sizes(n, t):
    rc = min(512, n)
    tq = min(128, n)
    tt = min(512, t)
    et = 512
    assert n % rc == 0 and n % tq == 0 and n % GRID_W == 0 and t % tt == 0 and tt % LANES == 0
    return rc, tq, tt, et


def _encode_all(x, meta_tokens, norm_mix_g, w_in, conv_w, q_norm_g, k_norm_g, conv_out_g, attn_out_g,
                w_out, norm_ffn_g, peer_wq, peer_subkeys, peer_u, peer_v):
    b, n, _ = x.shape
    t = b * n
    rc, tq, tt, et = _tile_sizes(n, t)
    cparams = functools.partial(pltpu.CompilerParams, vmem_limit_bytes=VMEM_LIMIT)

    w_in_b = w_in.astype(BF16)
    w_out_b = w_out.astype(BF16)
    g_mix = norm_mix_g.reshape(1, D_MODEL)
    qg = jnp.tile(q_norm_g, N_HEADS).reshape(1, ATTN_WIDTH)
    kg = jnp.tile(k_norm_g, N_KV_HEADS).reshape(1, KV_WIDTH)
    cog = conv_out_g.reshape(1, CONV_WIDTH)
    gmat = jnp.kron(jnp.eye(CONV_GROUPS, dtype=F32), jnp.full((HEAD_DIM, HEAD_DIM), 1.0 / HEAD_DIM, F32))
    cos_t, sin_t = _rope_tables(n)

    u_meta, k_meta, v_meta = pl.pallas_call(
        _meta_proj_kernel,
        out_shape=(jax.ShapeDtypeStruct((N_META, CONV_WIDTH), F32),
                   jax.ShapeDtypeStruct((N_KV_HEADS, N_META, HEAD_DIM), BF16),
                   jax.ShapeDtypeStruct((N_KV_HEADS, N_META, HEAD_DIM), BF16)),
        compiler_params=cparams(),
        name="meta_proj",
    )(meta_tokens, g_mix, w_in_b, kg, gmat[0:KV_WIDTH, 0:KV_WIDTH])
    pad = ((0, 0), (0, META_PAD - N_META), (0, 0))
    k_mp = jnp.pad(k_meta, pad)
    v_mp = jnp.pad(v_meta, pad)
    meta_bias = jnp.where(jnp.arange(META_PAD) < N_META, 0.0, NEG_BIG).astype(F32).reshape(1, META_PAD)

    const2 = lambda i, j: (0, 0)
    yconv, q, k, v = pl.pallas_call(
        functools.partial(_in_proj_kernel, n=n, rc=rc),
        grid=(b, n // rc),
        in_specs=[
            pl.BlockSpec((1, rc, D_MODEL), lambda i, j: (i, j, 0)),
            pl.BlockSpec((1, D_MODEL), const2),
            pl.BlockSpec((D_MODEL, IN_WIDTH), const2),
            pl.BlockSpec((3, CONV_WIDTH), const2),
            pl.BlockSpec((1, ATTN_WIDTH), const2),
            pl.BlockSpec((1, KV_WIDTH), const2),
            pl.BlockSpec((1, CONV_WIDTH), const2),
            pl.BlockSpec((rc, LANES), lambda i, j: (j, 0)),
            pl.BlockSpec((rc, LANES), lambda i, j: (j, 0)),
            pl.BlockSpec((1, CONV_WIDTH), const2),
            pl.BlockSpec((CONV_WIDTH, CONV_WIDTH), const2),
        ],
        out_specs=(
            pl.BlockSpec((1, n, CONV_WIDTH), lambda i, j: (i, 0, 0)),
            pl.BlockSpec((1, N_HEADS, rc, HEAD_DIM), lambda i, j: (i, 0, j, 0)),
            pl.BlockSpec((1, N_KV_HEADS, rc, HEAD_DIM), lambda i, j: (i, 0, j, 0)),
            pl.BlockSpec((1, N_KV_HEADS, rc, HEAD_DIM), lambda i, j: (i, 0, j, 0)),
        ),
        out_shape=(
            jax.ShapeDtypeStruct((b, n, CONV_WIDTH), BF16),
            jax.ShapeDtypeStruct((b, N_HEADS, n, HEAD_DIM), BF16),
            jax.ShapeDtypeStruct((b, N_KV_HEADS, n, HEAD_DIM), BF16),
            jax.ShapeDtypeStruct((b, N_KV_HEADS, n, HEAD_DIM), BF16),
        ),
        scratch_shapes=[pltpu.VMEM((n + 2 * SUBLANES, CONV_WIDTH), F32), pltpu.VMEM((n, CONV_WIDTH), F32)],
        compiler_params=cparams(dimension_semantics=("parallel", "arbitrary")),
        name="in_proj",
    )(x, g_mix, w_in_b, conv_w, qg, kg, cog, cos_t, sin_t, u_meta[N_META - 1:N_META], gmat)

    h = pl.pallas_call(
        functools.partial(_attn_out_kernel, tq=tq),
        grid=(b, n // tq),
        in_specs=[
            pl.BlockSpec((1, tq, D_MODEL), lambda i, j: (i, j, 0)),
            pl.BlockSpec((1, tq, CONV_WIDTH), lambda i, j: (i, j, 0)),
            pl.BlockSpec((1, N_HEADS, tq, HEAD_DIM), lambda i, j: (i, 0, j, 0)),
            pl.BlockSpec((1, N_KV_HEADS, n, HEAD_DIM), lambda i, j: (i, 0, 0, 0)),
            pl.BlockSpec((1, N_KV_HEADS, n, HEAD_DIM), lambda i, j: (i, 0, 0, 0)),
            pl.BlockSpec((N_KV_HEADS, META_PAD, HEAD_DIM), lambda i, j: (0, 0, 0)),
            pl.BlockSpec((N_KV_HEADS, META_PAD, HEAD_DIM), lambda i, j: (0, 0, 0)),
            pl.BlockSpec((1, META_PAD), const2),
            pl.BlockSpec((N_HEADS, HEAD_DIM), const2),
            pl.BlockSpec((CONV_WIDTH + ATTN_WIDTH, D_MODEL), const2),
        ],
        out_specs=pl.BlockSpec((1, tq, D_MODEL), lambda i, j: (i, j, 0)),
        out_shape=jax.ShapeDtypeStruct((b, n, D_MODEL), F32),
        compiler_params=cparams(dimension_semantics=("parallel", "parallel")),
        name="attn_out",
    )(x, yconv, q, k, v, k_mp, v_mp, meta_bias, attn_out_g.reshape(N_HEADS, HEAD_DIM), w_out_b)

    wq_t = peer_wq.T.astype(BF16)
    sk = peer_subkeys.reshape(2 * PEER_HEADS, PEER_NKEYS, PEER_HALF).astype(BF16)
    u_b = peer_u.astype(BF16)
    v_t = peer_v.T.astype(BF16)
    nc = tt // LANES
    sel_scr = pltpu.VMEM((PEER_HEADS, nc, PEER_NKEYS, LANES), F32)
    out = pl.pallas_call(
        functools.partial(_peer_kernel, tt=tt, et=et),
        grid=(t // tt, PEER_EXPERTS // et),
        in_specs=[
            pl.BlockSpec((tt, D_MODEL), lambda i, j: (i, 0)),
            pl.BlockSpec((1, D_MODEL), const2),
            pl.BlockSpec((PEER_HEADS * PEER_QDIM, D_MODEL), const2),
            pl.BlockSpec((2 * PEER_HEADS, PEER_NKEYS, PEER_HALF), lambda i, j: (0, 0, 0)),
            pl.BlockSpec((et, D_MODEL), lambda i, j: (j, 0)),
            pl.BlockSpec((D_MODEL, et), lambda i, j: (0, j)),
        ],
        out_specs=pl.BlockSpec((tt, D_MODEL), lambda i, j: (i, 0)),
        out_shape=jax.ShapeDtypeStruct((t, D_MODEL), F32),
        scratch_shapes=[
            pltpu.VMEM((D_MODEL, tt), BF16),
            pltpu.VMEM((2 * PEER_HEADS, nc, PEER_NKEYS, LANES), F32),
            sel_scr, sel_scr, sel_scr, sel_scr,
            pltpu.VMEM((D_MODEL, tt), F32),
            pltpu.VMEM((et, tt), BF16),
        ],
        compiler_params=cparams(dimension_semantics=("parallel", "arbitrary")),
        name="peer",
    )(h.reshape(t, D_MODEL), norm_ffn_g.reshape(1, D_MODEL), wq_t, sk, u_b, v_t)
    return out.reshape(b, n, D_MODEL)


def kernel(x_prompt, x_sample, meta_tokens, norm_mix_g, w_in, conv_w, q_norm_g, k_norm_g, conv_out_g,
           attn_out_g, w_out, norm_ffn_g, peer_wq, peer_subkeys, peer_u, peer_v):
    assert x_prompt.shape[1:] == x_sample.shape[1:]
    x = jnp.concatenate([x_prompt, x_sample], axis=0)
    y = _encode_all(x, meta_tokens, norm_mix_g[0], w_in[0], conv_w[0], q_norm_g[0], k_norm_g[0],
                    conv_out_g[0], attn_out_g[0], w_out[0], norm_ffn_g[0], peer_wq[0], peer_subkeys[0],
                    peer_u[0], peer_v[0])
    return y[:x_prompt.shape[0]], y[x_prompt.shape[0]:]
```

```python
import functools

import jax
import jax.numpy as jnp
from jax import lax
from jax.experimental import pallas as pl
from jax.experimental.pallas import tpu as pltpu

D_MODEL = 1024
N_META = 16
GRID_W = 64
CONV_WIDTH = 512
CONV_GROUPS = 8
N_HEADS = 8
N_KV_HEADS = 2
Q_PER_KV = N_HEADS // N_KV_HEADS
HEAD_DIM = 64
AXIS_DIM = HEAD_DIM // 2
ROPE_THETA = 10000.0
ATTN_WIDTH = N_HEADS * HEAD_DIM
KV_WIDTH = N_KV_HEADS * HEAD_DIM
IN_WIDTH = 3 * CONV_WIDTH + ATTN_WIDTH + 2 * KV_WIDTH
PEER_HEADS = 8
PEER_NKEYS = 128
PEER_EXPERTS = PEER_NKEYS * PEER_NKEYS
PEER_QDIM = 256
PEER_HALF = PEER_QDIM // 2
PEER_TOPK = 16
EPS = 1e-6

LANES = 128
SUBLANES = 8
META_PAD = LANES
NEG_BIG = -1e30
NOT_RANKED = 99.0
VMEM_LIMIT = 56 * 1024 * 1024

F32 = jnp.float32
BF16 = jnp.bfloat16
HI = lax.Precision.HIGHEST


def _rms_rows(x, g):
    return x * lax.rsqrt(jnp.mean(x * x, axis=-1, keepdims=True) + EPS) * g


def _group_rms(x, gmat, g):
    ms = jnp.dot(x * x, gmat, precision=HI, preferred_element_type=F32)
    return x * lax.rsqrt(ms + EPS) * g


def _rope(x, cos, sin):
    lane = lax.broadcasted_iota(jnp.int32, x.shape, 1)
    swapped = jnp.where(lane % (2 * (AXIS_DIM // 2)) < AXIS_DIM // 2,
                        pltpu.roll(x, LANES - AXIS_DIM // 2, 1), pltpu.roll(x, AXIS_DIM // 2, 1))
    return x * cos + swapped * sin


def _meta_proj_kernel(m_ref, g_ref, w_ref, kg_ref, gm_ref, u_ref, k_ref, v_ref):
    xn = _rms_rows(m_ref[...], g_ref[...]).astype(BF16)
    z = jnp.dot(xn, w_ref[...], preferred_element_type=F32)
    u_ref[...] = z[:, CONV_WIDTH:2 * CONV_WIDTH] * z[:, 2 * CONV_WIDTH:3 * CONV_WIDTH]
    k0 = 3 * CONV_WIDTH + ATTN_WIDTH
    kn = _group_rms(z[:, k0:k0 + KV_WIDTH], gm_ref[...], kg_ref[...])
    for g in range(N_KV_HEADS):
        k_ref[g] = kn[:, g * HEAD_DIM:(g + 1) * HEAD_DIM].astype(BF16)
        v_ref[g] = z[:, k0 + KV_WIDTH + g * HEAD_DIM:k0 + KV_WIDTH + (g + 1) * HEAD_DIM].astype(BF16)


def _in_proj_kernel(x_ref, g_ref, w_ref, cw_ref, qg_ref, kg_ref, cog_ref, cos_ref, sin_ref, um_ref,
                    gm_ref, yconv_ref, q_ref, k_ref, v_ref, u_scr, gb_scr, *, n, rc):
    c = pl.program_id(1)
    r0 = pl.multiple_of(c * rc, rc)
    halo = SUBLANES

    @pl.when(c == 0)
    def _():
        u_scr[0:halo, :] = jnp.broadcast_to(um_ref[...], (halo, CONV_WIDTH))
        u_scr[halo + n:2 * halo + n, :] = jnp.zeros((halo, CONV_WIDTH), F32)

    xn = _rms_rows(x_ref[0], g_ref[...]).astype(BF16)
    z = jnp.dot(xn, w_ref[...], preferred_element_type=F32)
    gb_scr[pl.ds(r0, rc), :] = z[:, 0:CONV_WIDTH]
    u_scr[pl.ds(halo + r0, rc), :] = z[:, CONV_WIDTH:2 * CONV_WIDTH] * z[:, 2 * CONV_WIDTH:3 * CONV_WIDTH]

    cos = cos_ref[...]
    sin = sin_ref[...]
    q0 = 3 * CONV_WIDTH
    qn = _group_rms(z[:, q0:q0 + ATTN_WIDTH], gm_ref[...], qg_ref[...])
    for j in range(ATTN_WIDTH // LANES):
        qr = _rope(qn[:, j * LANES:(j + 1) * LANES], cos, sin) * (HEAD_DIM ** -0.5)
        q_ref[0, 2 * j] = qr[:, :HEAD_DIM].astype(BF16)
        q_ref[0, 2 * j + 1] = qr[:, HEAD_DIM:].astype(BF16)
    k0 = q0 + ATTN_WIDTH
    kr = _rope(_group_rms(z[:, k0:k0 + KV_WIDTH], gm_ref[0:KV_WIDTH, 0:KV_WIDTH], kg_ref[...]), cos, sin)
    for g in range(N_KV_HEADS):
        k_ref[0, g] = kr[:, g * HEAD_DIM:(g + 1) * HEAD_DIM].astype(BF16)
        v_ref[0, g] = z[:, k0 + KV_WIDTH + g * HEAD_DIM:k0 + KV_WIDTH + (g + 1) * HEAD_DIM].astype(BF16)

    @pl.when(c == pl.num_programs(1) - 1)
    def _():
        for j in range(n // rc):
            s = j * rc
            um = u_scr[halo + s - 1:halo + s - 1 + rc, :]
            uc = u_scr[halo + s:halo + s + rc, :]
            up = u_scr[halo + s + 1:halo + s + 1 + rc, :]
            y = gb_scr[s:s + rc, :] * (um * cw_ref[0:1, :] + uc * cw_ref[1:2, :] + up * cw_ref[2:3, :])
            yconv_ref[0, s:s + rc, :] = _group_rms(y, gm_ref[...], cog_ref[...]).astype(BF16)


def _attn_out_kernel(x_ref, yconv_ref, q_ref, k_ref, v_ref, km_ref, vm_ref, bias_ref, ag_ref, wo_ref,
                     h_ref, *, tq):
    acc = x_ref[0] + jnp.dot(yconv_ref[0], wo_ref[0:CONV_WIDTH, :], preferred_element_type=F32)
    dn = (((1,), (1,)), ((), ()))
    for g in range(N_KV_HEADS):
        qg = q_ref[0, g * Q_PER_KV:(g + 1) * Q_PER_KV].reshape(Q_PER_KV * tq, HEAD_DIM)
        s = lax.dot_general(qg, k_ref[0, g], dn, preferred_element_type=F32)
        sm = lax.dot_general(qg, km_ref[g], dn, preferred_element_type=F32) + bias_ref[...]
        m = jnp.maximum(jnp.max(s, axis=-1, keepdims=True), jnp.max(sm, axis=-1, keepdims=True))
        p = jnp.exp(s - m)
        pm = jnp.exp(sm - m)
        l = jnp.sum(p, axis=-1, keepdims=True) + jnp.sum(pm, axis=-1, keepdims=True)
        o = (jnp.dot(p.astype(BF16), v_ref[0, g], preferred_element_type=F32)
             + jnp.dot(pm.astype(BF16), vm_ref[g], preferred_element_type=F32)) / l
        for r in range(Q_PER_KV):
            head = g * Q_PER_KV + r
            y = _rms_rows(o[r * tq:(r + 1) * tq], ag_ref[head:head + 1, :]).astype(BF16)
            w0 = CONV_WIDTH + head * HEAD_DIM
            acc = acc + jnp.dot(y, wo_ref[w0:w0 + HEAD_DIM, :], preferred_element_type=F32)
    h_ref[0] = acc


PACK = 16
N_CAND = 80


def _top16_ranks(s):
    rows = lax.broadcasted_iota(jnp.int32, s.shape, 0)
    r16 = lax.broadcasted_iota(jnp.int32, (PEER_TOPK, LANES), 0)
    rank = jnp.full(s.shape, NOT_RANKED, F32)
    vals = jnp.zeros((PEER_TOPK, LANES), F32)
    for a in range(PEER_TOPK):
        m = jnp.max(s, axis=0, keepdims=True)
        idx = jnp.min(jnp.where(s == m, rows, PEER_NKEYS), axis=0, keepdims=True)
        sel = rows == idx
        rank = jnp.where(sel, float(a), rank)
        s = jnp.where(sel, -jnp.inf, s)
        vals = jnp.where(r16 == a, m, vals)
    return rank, vals


def _top16_values(s):
    r16 = lax.broadcasted_iota(jnp.int32, (PEER_TOPK, LANES), 0)
    m = jnp.max(s, axis=0, keepdims=True)
    vals = jnp.broadcast_to(m, (PEER_TOPK, LANES))
    for a in range(1, PEER_TOPK):
        m = jnp.max(jnp.where(s < m, s, -jnp.inf), axis=0, keepdims=True)
        vals = jnp.where(r16 >= a, m, vals)
    return vals


def _pair_candidates(v1, v2):
    r8 = lax.broadcasted_iota(jnp.int32, (SUBLANES, LANES), 0)

    def bc(row):
        return jnp.broadcast_to(row, (SUBLANES, LANES))

    cands = [bc(v1[0:1]) + v2[0:8], bc(v1[0:1]) + v2[8:16]]
    poss = [r8, r8 + 8]
    for a in range(1, 8):
        cands.append(jnp.where(r8 < PEER_TOPK // (a + 1), bc(v1[a:a + 1]) + v2[0:8], -jnp.inf))
        poss.append(r8 + PEER_TOPK * a)
    cands.append(v1[8:16] + bc(v2[0:1]))
    poss.append((r8 + 8) * PEER_TOPK)
    return jnp.concatenate(cands, axis=0), jnp.concatenate(poss, axis=0)


def _pair_counts(cand0, sel):
    r8 = lax.broadcasted_iota(jnp.int32, (SUBLANES, LANES), 0)
    self32 = sel.astype(F32)
    z = jnp.sum(jnp.where(sel, jnp.exp(cand0 - cand0[0:1]), 0.0), axis=0, keepdims=True)
    n_lo = jnp.broadcast_to(jnp.sum(self32[0:16], axis=0, keepdims=True), (SUBLANES, LANES))
    for a in range(1, 8):
        na = jnp.sum(self32[8 + 8 * a:16 + 8 * a], axis=0, keepdims=True)
        n_lo = jnp.where(r8 == a, na, n_lo)
    return jnp.concatenate([n_lo, self32[N_CAND - 8:N_CAND]], axis=0), z


def _select_exact(s1, s2):
    rk1, v1 = _top16_ranks(s1)
    rk2, v2 = _top16_ranks(s2)
    cand0, pos = _pair_candidates(v1, v2)
    cand = cand0
    sel = jnp.zeros(cand.shape, jnp.bool_)
    for _ in range(PEER_TOPK):
        m = jnp.max(cand, axis=0, keepdims=True)
        first = jnp.min(jnp.where(cand == m, pos, PEER_TOPK * PEER_TOPK), axis=0, keepdims=True)
        hit = pos == first
        sel = jnp.logical_or(sel, hit)
        cand = jnp.where(hit, -jnp.inf, cand)
    counts, z = _pair_counts(cand0, sel)
    n_dense = jnp.zeros(s1.shape, F32)
    for a in range(PEER_TOPK):
        n_dense = jnp.where(rk1 == float(a), counts[a:a + 1], n_dense)
    return rk2, n_dense, z, v1[0:1], v2[0:1]


def _select_distinct(s1, s2):
    v1 = _top16_values(s1)
    v2 = _top16_values(s2)
    rk2 = jnp.zeros(s2.shape, F32)
    for b in range(PEER_TOPK):
        rk2 = jnp.where(s2 < v2[b:b + 1], float(b + 1), rk2)
    cand0, _ = _pair_candidates(v1, v2)
    m = cand0[0:1]
    for _ in range(1, PEER_TOPK):
        m = jnp.max(jnp.where(cand0 < m, cand0, -jnp.inf), axis=0, keepdims=True)
    sel = cand0 >= m
    counts, z = _pair_counts(cand0, sel)
    n_dense = jnp.zeros(s1.shape, F32)
    for a in range(PEER_TOPK):
        n_dense = jnp.where(s1 == v1[a:a + 1], counts[a:a + 1], n_dense)
    k = float(PEER_TOPK)
    n1 = jnp.sum((s1 >= v1[PEER_TOPK - 1:PEER_TOPK]).astype(F32), axis=0, keepdims=True)
    n2 = jnp.sum((rk2 < k).astype(F32), axis=0, keepdims=True)
    n3 = jnp.sum(sel.astype(F32), axis=0, keepdims=True)
    bad = jnp.logical_or(jnp.logical_or(n1 != k, n2 != k), n3 != k)
    return (rk2, n_dense, z, v1[0:1], v2[0:1]), jnp.logical_not(jnp.any(bad))


def _peer_kernel(h_ref, g_ref, wqt_ref, sk_ref, u_ref, vta_ref, vtb_ref, vtl_ref, out_ref,
                 xnt_scr, sc_scr, rk2_scr, e2_scr, n_scr, r_scr, acc_scr, pa_scr, pb_scr, *, tt, et):
    j = pl.program_id(1)
    last = pl.num_programs(1) - 1
    nc = tt // LANES
    eh = et // 2
    n_ih = eh // PEER_NKEYS
    n_pk = PEER_NKEYS // PACK

    @pl.when(j == 0)
    def _prologue():
        xn = _rms_rows(h_ref[...], g_ref[...])
        xnt = xn.T.astype(BF16)
        xnt_scr[...] = xnt
        qt = jnp.dot(wqt_ref[...], xnt, preferred_element_type=F32).astype(BF16)
        for hp in range(2 * PEER_HEADS):
            st = jnp.dot(sk_ref[hp], qt[hp * PEER_HALF:(hp + 1) * PEER_HALF, :],
                         preferred_element_type=F32)
            for c in range(nc):
                sc_scr[hp, c] = st[:, c * LANES:(c + 1) * LANES]

        def head_body(hh, carry):
            def chunk_body(c, carry2):
                s1 = sc_scr[2 * hh, c]
                s2 = sc_scr[2 * hh + 1, c]

                def publish(rk2, n_dense, z, m1, m2):
                    rk2_scr[hh, c] = rk2.astype(BF16).reshape(n_pk, PACK, LANES)
                    e2_scr[hh, c] = jnp.exp(s2 - m2).astype(BF16).reshape(n_pk, PACK, LANES)
                    n_scr[hh, c] = n_dense
                    r_scr[hh, c] = jnp.where(n_dense > 0.0, jnp.exp(s1 - m1) / z, 0.0)

                picked, distinct = _select_distinct(s1, s2)
                publish(*picked)

                @pl.when(jnp.logical_not(distinct))
                def _():
                    publish(*_select_exact(s1, s2))
                return carry2
            return lax.fori_loop(0, nc, chunk_body, carry)
        lax.fori_loop(0, PEER_HEADS, head_body, 0)
        acc_scr[...] = jnp.zeros(acc_scr.shape, F32)
        pb_scr[...] = jnp.zeros((eh, tt), BF16)

    def build(half, p_ref):
        a_t = jnp.dot(u_ref[half * eh:(half + 1) * eh, :], xnt_scr[...], preferred_element_type=F32)
        hid = (0.5 * a_t * (1.0 + lax.erf(a_t * (2.0 ** -0.5)))).astype(BF16)
        for ii in range(n_ih):
            i = (2 * j + half) * n_ih + ii
            for c in range(nc):
                w = jnp.zeros((n_pk, PACK, LANES), BF16)
                for hh in range(PEER_HEADS):
                    n_row = jnp.broadcast_to(n_scr[hh, c, pl.ds(i, 1), :], (PACK, LANES)).astype(BF16)
                    r_row = jnp.broadcast_to(r_scr[hh, c, pl.ds(i, 1), :], (PACK, LANES)).astype(BF16)
                    w = w + jnp.where(rk2_scr[hh, c] < n_row, e2_scr[hh, c], jnp.zeros((), BF16)) * r_row
                rows = slice(ii * PEER_NKEYS, (ii + 1) * PEER_NKEYS)
                cols = slice(c * LANES, (c + 1) * LANES)
                p_ref[rows, cols] = hid[rows, cols] * w.reshape(PEER_NKEYS, LANES)

    build(0, pa_scr)
    acc_scr[...] += jnp.dot(vtb_ref[...], pb_scr[...], preferred_element_type=F32)
    build(1, pb_scr)
    acc_scr[...] += jnp.dot(vta_ref[...], pa_scr[...], preferred_element_type=F32)

    @pl.when(j == last)
    def _epilogue():
        tail = jnp.dot(vtl_ref[...], pb_scr[...], preferred_element_type=F32)
        out_ref[...] = h_ref[...] + (acc_scr[...] + tail).T


def _rope_tables(n):
    rows = n // GRID_W
    row = jnp.repeat(jnp.arange(rows, dtype=jnp.int32), GRID_W).astype(F32)
    col = jnp.tile(jnp.arange(GRID_W, dtype=jnp.int32), rows).astype(F32)
    freqs = ROPE_THETA ** (-jnp.arange(0, AXIS_DIM, 2, dtype=F32) / AXIS_DIM)
    ang_r = row[:, None] * freqs[None, :]
    ang_c = col[:, None] * freqs[None, :]
    cos_h = jnp.concatenate([jnp.cos(ang_r)] * 2 + [jnp.cos(ang_c)] * 2, axis=-1)
    sin_h = jnp.concatenate([-jnp.sin(ang_r), jnp.sin(ang_r), -jnp.sin(ang_c), jnp.sin(ang_c)], axis=-1)
    return jnp.tile(cos_h, (1, LANES // HEAD_DIM)), jnp.tile(sin_h, (1, LANES // HEAD_DIM))


def _tile_sizes(n, t):
    rc = min(512, n)
    tq = min(128, n)
    tt = min(512, t)
    et = 1024
    assert n % rc == 0 and n % tq == 0 and n % GRID_W == 0 and t % tt == 0 and tt % LANES == 0
    return rc, tq, tt, et


def _encode_all(x, meta_tokens, norm_mix_g, w_in, conv_w, q_norm_g, k_norm_g, conv_out_g, attn_out_g,
                w_out, norm_ffn_g, peer_wq, peer_subkeys, peer_u, peer_v):
    b, n, _ = x.shape
    t = b * n
    rc, tq, tt, et = _tile_sizes(n, t)
    cparams = functools.partial(pltpu.CompilerParams, vmem_limit_bytes=VMEM_LIMIT)

    w_in_b = w_in.astype(BF16)
    w_out_b = w_out.astype(BF16)
    g_mix = norm_mix_g.reshape(1, D_MODEL)
    qg = jnp.tile(q_norm_g, N_HEADS).reshape(1, ATTN_WIDTH)
    kg = jnp.tile(k_norm_g, N_KV_HEADS).reshape(1, KV_WIDTH)
    cog = conv_out_g.reshape(1, CONV_WIDTH)
    gmat = jnp.kron(jnp.eye(CONV_GROUPS, dtype=F32), jnp.full((HEAD_DIM, HEAD_DIM), 1.0 / HEAD_DIM, F32))
    cos_t, sin_t = _rope_tables(n)

    u_meta, k_meta, v_meta = pl.pallas_call(
        _meta_proj_kernel,
        out_shape=(jax.ShapeDtypeStruct((N_META, CONV_WIDTH), F32),
                   jax.ShapeDtypeStruct((N_KV_HEADS, N_META, HEAD_DIM), BF16),
                   jax.ShapeDtypeStruct((N_KV_HEADS, N_META, HEAD_DIM), BF16)),
        compiler_params=cparams(),
        name="meta_proj",
    )(meta_tokens, g_mix, w_in_b, kg, gmat[0:KV_WIDTH, 0:KV_WIDTH])
    pad = ((0, 0), (0, META_PAD - N_META), (0, 0))
    k_mp = jnp.pad(k_meta, pad)
    v_mp = jnp.pad(v_meta, pad)
    meta_bias = jnp.where(jnp.arange(META_PAD) < N_META, 0.0, NEG_BIG).astype(F32).reshape(1, META_PAD)

    const2 = lambda i, j: (0, 0)
    yconv, q, k, v = pl.pallas_call(
        functools.partial(_in_proj_kernel, n=n, rc=rc),
        grid=(b, n // rc),
        in_specs=[
            pl.BlockSpec((1, rc, D_MODEL), lambda i, j: (i, j, 0)),
            pl.BlockSpec((1, D_MODEL), const2),
            pl.BlockSpec((D_MODEL, IN_WIDTH), const2),
            pl.BlockSpec((3, CONV_WIDTH), const2),
            pl.BlockSpec((1, ATTN_WIDTH), const2),
            pl.BlockSpec((1, KV_WIDTH), const2),
            pl.BlockSpec((1, CONV_WIDTH), const2),
            pl.BlockSpec((rc, LANES), lambda i, j: (j, 0)),
            pl.BlockSpec((rc, LANES), lambda i, j: (j, 0)),
            pl.BlockSpec((1, CONV_WIDTH), const2),
            pl.BlockSpec((CONV_WIDTH, CONV_WIDTH), const2),
        ],
        out_specs=(
            pl.BlockSpec((1, n, CONV_WIDTH), lambda i, j: (i, 0, 0)),
            pl.BlockSpec((1, N_HEADS, rc, HEAD_DIM), lambda i, j: (i, 0, j, 0)),
            pl.BlockSpec((1, N_KV_HEADS, rc, HEAD_DIM), lambda i, j: (i, 0, j, 0)),
            pl.BlockSpec((1, N_KV_HEADS, rc, HEAD_DIM), lambda i, j: (i, 0, j, 0)),
        ),
        out_shape=(
            jax.ShapeDtypeStruct((b, n, CONV_WIDTH), BF16),
            jax.ShapeDtypeStruct((b, N_HEADS, n, HEAD_DIM), BF16),
            jax.ShapeDtypeStruct((b, N_KV_HEADS, n, HEAD_DIM), BF16),
            jax.ShapeDtypeStruct((b, N_KV_HEADS, n, HEAD_DIM), BF16),
        ),
        scratch_shapes=[pltpu.VMEM((n + 2 * SUBLANES, CONV_WIDTH), F32), pltpu.VMEM((n, CONV_WIDTH), F32)],
        compiler_params=cparams(dimension_semantics=("parallel", "arbitrary")),
        name="in_proj",
    )(x, g_mix, w_in_b, conv_w, qg, kg, cog, cos_t, sin_t, u_meta[N_META - 1:N_META], gmat)

    h = pl.pallas_call(
        functools.partial(_attn_out_kernel, tq=tq),
        grid=(b, n // tq),
        in_specs=[
            pl.BlockSpec((1, tq, D_MODEL), lambda i, j: (i, j, 0)),
            pl.BlockSpec((1, tq, CONV_WIDTH), lambda i, j: (i, j, 0)),
            pl.BlockSpec((1, N_HEADS, tq, HEAD_DIM), lambda i, j: (i, 0, j, 0)),
            pl.BlockSpec((1, N_KV_HEADS, n, HEAD_DIM), lambda i, j: (i, 0, 0, 0)),
            pl.BlockSpec((1, N_KV_HEADS, n, HEAD_DIM), lambda i, j: (i, 0, 0, 0)),
            pl.BlockSpec((N_KV_HEADS, META_PAD, HEAD_DIM), lambda i, j: (0, 0, 0)),
            pl.BlockSpec((N_KV_HEADS, META_PAD, HEAD_DIM), lambda i, j: (0, 0, 0)),
            pl.BlockSpec((1, META_PAD), const2),
            pl.BlockSpec((N_HEADS, HEAD_DIM), const2),
            pl.BlockSpec((CONV_WIDTH + ATTN_WIDTH, D_MODEL), const2),
        ],
        out_specs=pl.BlockSpec((1, tq, D_MODEL), lambda i, j: (i, j, 0)),
        out_shape=jax.ShapeDtypeStruct((b, n, D_MODEL), F32),
        compiler_params=cparams(dimension_semantics=("parallel", "parallel")),
        name="attn_out",
    )(x, yconv, q, k, v, k_mp, v_mp, meta_bias, attn_out_g.reshape(N_HEADS, HEAD_DIM), w_out_b)

    wq_t = peer_wq.T.astype(BF16)
    sk = peer_subkeys.reshape(2 * PEER_HEADS, PEER_NKEYS, PEER_HALF).astype(BF16)
    u_b = peer_u.astype(BF16)
    v_t = peer_v.T.astype(BF16)
    nc = tt // LANES
    n_eb = PEER_EXPERTS // et
    sel_f32 = pltpu.VMEM((PEER_HEADS, nc, PEER_NKEYS, LANES), F32)
    sel_b16 = pltpu.VMEM((PEER_HEADS, nc, PEER_NKEYS // PACK, PACK, LANES), BF16)
    out = pl.pallas_call(
        functools.partial(_peer_kernel, tt=tt, et=et),
        grid=(t // tt, n_eb),
        in_specs=[
            pl.BlockSpec((tt, D_MODEL), lambda i, j: (i, 0)),
            pl.BlockSpec((1, D_MODEL), const2),
            pl.BlockSpec((PEER_HEADS * PEER_QDIM, D_MODEL), const2),
            pl.BlockSpec((2 * PEER_HEADS, PEER_NKEYS, PEER_HALF), lambda i, j: (0, 0, 0)),
            pl.BlockSpec((et, D_MODEL), lambda i, j: (j, 0)),
            pl.BlockSpec((D_MODEL, et // 2), lambda i, j: (0, 2 * j)),
            pl.BlockSpec((D_MODEL, et // 2), lambda i, j: (0, jnp.maximum(2 * j - 1, 0))),
            pl.BlockSpec((D_MODEL, et // 2), lambda i, j: (0, 2 * n_eb - 1)),
        ],
        out_specs=pl.BlockSpec((tt, D_MODEL), lambda i, j: (i, 0)),
        out_shape=jax.ShapeDtypeStruct((t, D_MODEL), F32),
        scratch_shapes=[
            pltpu.VMEM((D_MODEL, tt), BF16),
            pltpu.VMEM((2 * PEER_HEADS, nc, PEER_NKEYS, LANES), F32),
            sel_b16, sel_b16, sel_f32, sel_f32,
            pltpu.VMEM((D_MODEL, tt), F32),
            pltpu.VMEM((et // 2, tt), BF16),
            pltpu.VMEM((et // 2, tt), BF16),
        ],
        compiler_params=cparams(dimension_semantics=("parallel", "arbitrary")),
        name="peer",
    )(h.reshape(t, D_MODEL), norm_ffn_g.reshape(1, D_MODEL), wq_t, sk, u_b, v_t, v_t, v_t)
    return out.reshape(b, n, D_MODEL)


def kernel(x_prompt, x_sample, meta_tokens, norm_mix_g, w_in, conv_w, q_norm_g, k_norm_g, conv_out_g,
           attn_out_g, w_out, norm_ffn_g, peer_wq, peer_subkeys, peer_u, peer_v):
    assert x_prompt.shape[1:] == x_sample.shape[1:]
    x = jnp.concatenate([x_prompt, x_sample], axis=0)
    y = _encode_all(x, meta_tokens, norm_mix_g[0], w_in[0], conv_w[0], q_norm_g[0], k_norm_g[0],
                    conv_out_g[0], attn_out_g[0], w_out[0], norm_ffn_g[0], peer_wq[0], peer_subkeys[0],
                    peer_u[0], peer_v[0])
    return y[:x_prompt.shape[0]], y[x_prompt.shape[0]:]
```

```python
import functools

import jax
import jax.numpy as jnp
from jax import lax
from jax.experimental import pallas as pl
from jax.experimental.pallas import tpu as pltpu

D_MODEL = 1024
N_META = 16
GRID_W = 64
CONV_WIDTH = 512
CONV_GROUPS = 8
N_HEADS = 8
N_KV_HEADS = 2
Q_PER_KV = N_HEADS // N_KV_HEADS
HEAD_DIM = 64
AXIS_DIM = HEAD_DIM // 2
ROPE_THETA = 10000.0
ATTN_WIDTH = N_HEADS * HEAD_DIM
KV_WIDTH = N_KV_HEADS * HEAD_DIM
IN_WIDTH = 3 * CONV_WIDTH + ATTN_WIDTH + 2 * KV_WIDTH
PEER_HEADS = 8
PEER_NKEYS = 128
PEER_EXPERTS = PEER_NKEYS * PEER_NKEYS
PEER_QDIM = 256
PEER_HALF = PEER_QDIM // 2
PEER_TOPK = 16
EPS = 1e-6

LANES = 128
SUBLANES = 8
META_PAD = LANES
NEG_BIG = -1e30
LOG2E = 1.4426950408889634
NOT_RANKED = 99.0
VMEM_LIMIT = 56 * 1024 * 1024

F32 = jnp.float32
BF16 = jnp.bfloat16
HI = lax.Precision.HIGHEST


def _rms_rows(x, g):
    return x * lax.rsqrt(jnp.mean(x * x, axis=-1, keepdims=True) + EPS) * g


def _group_rms(x, gmat, g):
    ms = jnp.dot(x * x, gmat, precision=HI, preferred_element_type=F32)
    return x * lax.rsqrt(ms + EPS) * g


def _rope(x, cos, sin):
    lane = lax.broadcasted_iota(jnp.int32, x.shape, 1)
    swapped = jnp.where(lane % (2 * (AXIS_DIM // 2)) < AXIS_DIM // 2,
                        pltpu.roll(x, LANES - AXIS_DIM // 2, 1), pltpu.roll(x, AXIS_DIM // 2, 1))
    return x * cos + swapped * sin


def _meta_proj_kernel(m_ref, g_ref, w_ref, kg_ref, gm_ref, u_ref, k_ref, v_ref):
    xn = _rms_rows(m_ref[...], g_ref[...]).astype(BF16)
    z = jnp.dot(xn, w_ref[...], preferred_element_type=F32)
    u_ref[...] = z[:, CONV_WIDTH:2 * CONV_WIDTH] * z[:, 2 * CONV_WIDTH:3 * CONV_WIDTH]
    k0 = 3 * CONV_WIDTH + ATTN_WIDTH
    kn = _group_rms(z[:, k0:k0 + KV_WIDTH], gm_ref[...], kg_ref[...])
    for g in range(N_KV_HEADS):
        k_ref[g] = kn[:, g * HEAD_DIM:(g + 1) * HEAD_DIM].astype(BF16)
        v_ref[g] = z[:, k0 + KV_WIDTH + g * HEAD_DIM:k0 + KV_WIDTH + (g + 1) * HEAD_DIM].astype(BF16)


def _in_proj_kernel(xp_ref, xs_ref, g_ref, w_ref, cw_ref, qg_ref, kg_ref, cog_ref, cos_ref, sin_ref, um_ref,
                    gm_ref, yconv_ref, q_ref, k_ref, v_ref, u_scr, gb_scr, *, n, rc, bp):
    c = pl.program_id(1)
    r0 = pl.multiple_of(c * rc, rc)
    halo = SUBLANES

    @pl.when(c == 0)
    def _():
        u_scr[0:halo, :] = jnp.broadcast_to(um_ref[...], (halo, CONV_WIDTH))
        u_scr[halo + n:2 * halo + n, :] = jnp.zeros((halo, CONV_WIDTH), F32)

    x = jnp.where(pl.program_id(0) < bp, xp_ref[0], xs_ref[0])
    xn = _rms_rows(x, g_ref[...]).astype(BF16)
    z = jnp.dot(xn, w_ref[...], preferred_element_type=F32)
    gb_scr[pl.ds(r0, rc), :] = z[:, 0:CONV_WIDTH]
    u_scr[pl.ds(halo + r0, rc), :] = z[:, CONV_WIDTH:2 * CONV_WIDTH] * z[:, 2 * CONV_WIDTH:3 * CONV_WIDTH]

    cos = cos_ref[...]
    sin = sin_ref[...]
    q0 = 3 * CONV_WIDTH
    qn = _group_rms(z[:, q0:q0 + ATTN_WIDTH], gm_ref[...], qg_ref[...])
    for j in range(ATTN_WIDTH // LANES):
        qr = _rope(qn[:, j * LANES:(j + 1) * LANES], cos, sin) * (HEAD_DIM ** -0.5 * LOG2E)
        q_ref[0, 2 * j] = qr[:, :HEAD_DIM].astype(BF16)
        q_ref[0, 2 * j + 1] = qr[:, HEAD_DIM:].astype(BF16)
    k0 = q0 + ATTN_WIDTH
    kr = _rope(_group_rms(z[:, k0:k0 + KV_WIDTH], gm_ref[0:KV_WIDTH, 0:KV_WIDTH], kg_ref[...]), cos, sin)
    for g in range(N_KV_HEADS):
        k_ref[0, g] = kr[:, g * HEAD_DIM:(g + 1) * HEAD_DIM].astype(BF16)
        v_ref[0, g, :, 0:HEAD_DIM] = z[:, k0 + KV_WIDTH + g * HEAD_DIM:
                                       k0 + KV_WIDTH + (g + 1) * HEAD_DIM].astype(BF16)
        v_ref[0, g, :, HEAD_DIM:2 * HEAD_DIM] = jnp.ones((rc, HEAD_DIM), BF16)

    @pl.when(c == pl.num_programs(1) - 1)
    def _():
        for j in range(n // rc):
            s = j * rc
            um = u_scr[halo + s - 1:halo + s - 1 + rc, :]
            uc = u_scr[halo + s:halo + s + rc, :]
            up = u_scr[halo + s + 1:halo + s + 1 + rc, :]
            y = gb_scr[s:s + rc, :] * (um * cw_ref[0:1, :] + uc * cw_ref[1:2, :] + up * cw_ref[2:3, :])
            yconv_ref[0, s:s + rc, :] = _group_rms(y, gm_ref[...], cog_ref[...]).astype(BF16)


def _attn_out_kernel(xp_ref, xs_ref, yconv_ref, q_ref, k_ref, v_ref, km_ref, vm_ref, bias_ref, ag_ref, wo_ref,
                     h_ref, *, tq, bp):
    x = jnp.where(pl.program_id(0) < bp, xp_ref[0], xs_ref[0])
    acc = x + jnp.dot(yconv_ref[0], wo_ref[0:CONV_WIDTH, :], preferred_element_type=F32)
    dn = (((1,), (1,)), ((), ()))
    for g in range(N_KV_HEADS):
        qg = q_ref[0, g * Q_PER_KV:(g + 1) * Q_PER_KV].reshape(Q_PER_KV * tq, HEAD_DIM)
        s = lax.dot_general(qg, k_ref[0, g], dn, preferred_element_type=F32)
        sm = lax.dot_general(qg, km_ref[g], dn, preferred_element_type=F32) + bias_ref[...]
        m = jnp.maximum(jnp.max(s, axis=-1, keepdims=True), jnp.max(sm, axis=-1, keepdims=True))
        p = jnp.exp2(s - m).astype(BF16)
        pm = jnp.exp2(sm - m).astype(BF16)
        ol = (jnp.dot(p, v_ref[0, g], preferred_element_type=F32)
              + jnp.dot(pm, vm_ref[g], preferred_element_type=F32))
        o = ol[:, 0:HEAD_DIM] / ol[:, HEAD_DIM:HEAD_DIM + 1]
        for r in range(Q_PER_KV):
            head = g * Q_PER_KV + r
            y = _rms_rows(o[r * tq:(r + 1) * tq], ag_ref[head:head + 1, :]).astype(BF16)
            w0 = CONV_WIDTH + head * HEAD_DIM
            acc = acc + jnp.dot(y, wo_ref[w0:w0 + HEAD_DIM, :], preferred_element_type=F32)
    h_ref[0] = acc


PACK = 16
N_CAND = 80


def _top16_ranks(s):
    rows = lax.broadcasted_iota(jnp.int32, s.shape, 0)
    r16 = lax.broadcasted_iota(jnp.int32, (PEER_TOPK, LANES), 0)
    rank = jnp.full(s.shape, NOT_RANKED, F32)
    vals = jnp.zeros((PEER_TOPK, LANES), F32)
    for a in range(PEER_TOPK):
        m = jnp.max(s, axis=0, keepdims=True)
        idx = jnp.min(jnp.where(s == m, rows, PEER_NKEYS), axis=0, keepdims=True)
        sel = rows == idx
        rank = jnp.where(sel, float(a), rank)
        s = jnp.where(sel, -jnp.inf, s)
        vals = jnp.where(r16 == a, m, vals)
    return rank, vals


def _top16_values(s):
    r16 = lax.broadcasted_iota(jnp.int32, (PEER_TOPK, LANES), 0)
    m = jnp.max(s, axis=0, keepdims=True)
    vals = jnp.broadcast_to(m, (PEER_TOPK, LANES))
    for a in range(1, PEER_TOPK):
        m = jnp.max(jnp.where(s < m, s, -jnp.inf), axis=0, keepdims=True)
        vals = jnp.where(r16 >= a, m, vals)
    return vals


def _pair_candidates(v1, v2):
    r8 = lax.broadcasted_iota(jnp.int32, (SUBLANES, LANES), 0)

    def bc(row):
        return jnp.broadcast_to(row, (SUBLANES, LANES))

    cands = [bc(v1[0:1]) + v2[0:8], bc(v1[0:1]) + v2[8:16]]
    poss = [r8, r8 + 8]
    for a in range(1, 8):
        cands.append(jnp.where(r8 < PEER_TOPK // (a + 1), bc(v1[a:a + 1]) + v2[0:8], -jnp.inf))
        poss.append(r8 + PEER_TOPK * a)
    cands.append(v1[8:16] + bc(v2[0:1]))
    poss.append((r8 + 8) * PEER_TOPK)
    return jnp.concatenate(cands, axis=0), jnp.concatenate(poss, axis=0)


def _pair_counts(cand0, sel):
    r8 = lax.broadcasted_iota(jnp.int32, (SUBLANES, LANES), 0)
    self32 = sel.astype(F32)
    z = jnp.sum(jnp.where(sel, jnp.exp(cand0 - cand0[0:1]), 0.0), axis=0, keepdims=True)
    n_lo = jnp.broadcast_to(jnp.sum(self32[0:16], axis=0, keepdims=True), (SUBLANES, LANES))
    for a in range(1, 8):
        na = jnp.sum(self32[8 + 8 * a:16 + 8 * a], axis=0, keepdims=True)
        n_lo = jnp.where(r8 == a, na, n_lo)
    return jnp.concatenate([n_lo, self32[N_CAND - 8:N_CAND]], axis=0), z


def _select_exact(s1, s2):
    rk1, v1 = _top16_ranks(s1)
    rk2, v2 = _top16_ranks(s2)
    cand0, pos = _pair_candidates(v1, v2)
    cand = cand0
    sel = jnp.zeros(cand.shape, jnp.bool_)
    for _ in range(PEER_TOPK):
        m = jnp.max(cand, axis=0, keepdims=True)
        first = jnp.min(jnp.where(cand == m, pos, PEER_TOPK * PEER_TOPK), axis=0, keepdims=True)
        hit = pos == first
        sel = jnp.logical_or(sel, hit)
        cand = jnp.where(hit, -jnp.inf, cand)
    counts, z = _pair_counts(cand0, sel)
    n_dense = jnp.zeros(s1.shape, F32)
    for a in range(PEER_TOPK):
        n_dense = jnp.where(rk1 == float(a), counts[a:a + 1], n_dense)
    return rk2, n_dense, z, v1[0:1], v2[0:1]


def _select_distinct(s1, s2):
    v1 = _top16_values(s1)
    v2 = _top16_values(s2)
    rk2 = jnp.zeros(s2.shape, F32)
    for b in range(PEER_TOPK):
        rk2 = jnp.where(s2 < v2[b:b + 1], float(b + 1), rk2)
    cand0, _ = _pair_candidates(v1, v2)
    m = cand0[0:1]
    for _ in range(1, PEER_TOPK):
        m = jnp.max(jnp.where(cand0 < m, cand0, -jnp.inf), axis=0, keepdims=True)
    sel = cand0 >= m
    counts, z = _pair_counts(cand0, sel)
    n_dense = jnp.zeros(s1.shape, F32)
    for a in range(PEER_TOPK):
        n_dense = jnp.where(s1 == v1[a:a + 1], counts[a:a + 1], n_dense)
    k = float(PEER_TOPK)
    n1 = jnp.sum((s1 >= v1[PEER_TOPK - 1:PEER_TOPK]).astype(F32), axis=0, keepdims=True)
    n2 = jnp.sum((rk2 < k).astype(F32), axis=0, keepdims=True)
    n3 = jnp.sum(sel.astype(F32), axis=0, keepdims=True)
    bad = jnp.logical_or(jnp.logical_or(n1 != k, n2 != k), n3 != k)
    return (rk2, n_dense, z, v1[0:1], v2[0:1]), jnp.logical_not(jnp.any(bad))


def _peer_kernel(h_ref, g_ref, wqt_ref, sk_ref, u_ref, vtp_ref, vtl_ref, outp_ref, outs_ref,
                 xnt_scr, sc_scr, rk2_scr, e2_scr, n_scr, r_scr, acc_scr, pcur_scr, pprev_scr, hid_scr, *, tt, et, tp):
    j = pl.program_id(1)
    last = pl.num_programs(1) - 1
    nc = tt // LANES
    eh = et // 2
    n_ih = eh // PEER_NKEYS
    n_pk = PEER_NKEYS // PACK

    @pl.when(j == 0)
    def _prologue():
        xn = _rms_rows(h_ref[...], g_ref[...])
        xnt = xn.T.astype(BF16)
        xnt_scr[...] = xnt
        qt = jnp.dot(wqt_ref[...], xnt, preferred_element_type=F32).astype(BF16)
        for hp in range(2 * PEER_HEADS):
            st = jnp.dot(sk_ref[hp], qt[hp * PEER_HALF:(hp + 1) * PEER_HALF, :],
                         preferred_element_type=F32)
            for c in range(nc):
                sc_scr[hp, c] = st[:, c * LANES:(c + 1) * LANES]

        def head_body(hh, carry):
            def chunk_body(c, carry2):
                s1 = sc_scr[2 * hh, c]
                s2 = sc_scr[2 * hh + 1, c]

                def publish(rk2, n_dense, z, m1, m2):
                    rk2_scr[hh, c] = rk2.astype(BF16).reshape(n_pk, PACK, LANES)
                    e2_scr[hh, c] = jnp.exp(s2 - m2).astype(BF16).reshape(n_pk, PACK, LANES)
                    n_scr[hh, c] = n_dense
                    r_scr[hh, c] = jnp.where(n_dense > 0.0, jnp.exp(s1 - m1) / z, 0.0)

                picked, distinct = _select_distinct(s1, s2)
                publish(*picked)

                @pl.when(jnp.logical_not(distinct))
                def _():
                    publish(*_select_exact(s1, s2))
                return carry2
            return lax.fori_loop(0, nc, chunk_body, carry)
        lax.fori_loop(0, PEER_HEADS, head_body, 0)
        acc_scr[...] = jnp.zeros(acc_scr.shape, F32)
        pcur_scr[...] = jnp.zeros((et, tt), BF16)

    pprev_scr[...] = pcur_scr[...]

    def build(half):
        a_t = jnp.dot(u_ref[half * eh:(half + 1) * eh, :], xnt_scr[...], preferred_element_type=F32)
        hid_scr[half * eh:(half + 1) * eh, :] = (0.5 * a_t * (1.0 + lax.erf(a_t * (2.0 ** -0.5)))).astype(BF16)
        i0 = (2 * j + half) * n_ih
        for c in range(nc):
            ws = [jnp.zeros((n_pk, PACK, LANES), BF16) for _ in range(n_ih)]
            for hh in range(PEER_HEADS):
                rk2 = rk2_scr[hh, c]
                e2 = e2_scr[hh, c]
                for ii in range(n_ih):
                    n_row = jnp.broadcast_to(n_scr[hh, c, pl.ds(i0 + ii, 1), :], (PACK, LANES)).astype(BF16)
                    r_row = jnp.broadcast_to(r_scr[hh, c, pl.ds(i0 + ii, 1), :], (PACK, LANES)).astype(BF16)
                    ws[ii] = ws[ii] + jnp.where(rk2 < n_row, e2, jnp.zeros((), BF16)) * r_row
            cols = slice(c * LANES, (c + 1) * LANES)
            for ii in range(n_ih):
                rows = slice(half * eh + ii * PEER_NKEYS, half * eh + (ii + 1) * PEER_NKEYS)
                pcur_scr[rows, cols] = hid_scr[rows, cols] * ws[ii].reshape(PEER_NKEYS, LANES)

    build(0)
    build(1)
    acc_scr[...] += jnp.dot(vtp_ref[...], pprev_scr[...], preferred_element_type=F32)

    @pl.when(j == last)
    def _epilogue():
        tail = jnp.dot(vtl_ref[...], pcur_scr[...], preferred_element_type=F32)
        res = h_ref[...] + (acc_scr[...] + tail).T
        i = pl.program_id(0)

        @pl.when(i < tp)
        def _():
            outp_ref[...] = res

        @pl.when(i >= tp)
        def _():
            outs_ref[...] = res


def _rope_tables(n):
    rows = n // GRID_W
    row = jnp.repeat(jnp.arange(rows, dtype=jnp.int32), GRID_W).astype(F32)
    col = jnp.tile(jnp.arange(GRID_W, dtype=jnp.int32), rows).astype(F32)
    freqs = ROPE_THETA ** (-jnp.arange(0, AXIS_DIM, 2, dtype=F32) / AXIS_DIM)
    ang_r = row[:, None] * freqs[None, :]
    ang_c = col[:, None] * freqs[None, :]
    cos_h = jnp.concatenate([jnp.cos(ang_r)] * 2 + [jnp.cos(ang_c)] * 2, axis=-1)
    sin_h = jnp.concatenate([-jnp.sin(ang_r), jnp.sin(ang_r), -jnp.sin(ang_c), jnp.sin(ang_c)], axis=-1)
    return jnp.tile(cos_h, (1, LANES // HEAD_DIM)), jnp.tile(sin_h, (1, LANES // HEAD_DIM))


def _tile_sizes(n, t_prompt, t_sample):
    rc = min(512, n)
    tq = min(128, n)
    tt = min(512, t_prompt, t_sample)
    et = 1024
    assert n % rc == 0 and n % tq == 0 and n % GRID_W == 0 and tt % LANES == 0
    assert t_prompt % tt == 0 and t_sample % tt == 0
    return rc, tq, tt, et


def _encode_all(x_prompt, x_sample, meta_tokens, norm_mix_g, w_in, conv_w, q_norm_g, k_norm_g, conv_out_g, attn_out_g,
                w_out, norm_ffn_g, peer_wq, peer_subkeys, peer_u, peer_v):
    bp, n, _ = x_prompt.shape
    b = bp + x_sample.shape[0]
    t = b * n
    rc, tq, tt, et = _tile_sizes(n, bp * n, t - bp * n)
    cparams = functools.partial(pltpu.CompilerParams, vmem_limit_bytes=VMEM_LIMIT)

    w_in_b = w_in.astype(BF16)
    w_out_b = w_out.astype(BF16)
    g_mix = norm_mix_g.reshape(1, D_MODEL)
    qg = jnp.tile(q_norm_g, N_HEADS).reshape(1, ATTN_WIDTH)
    kg = jnp.tile(k_norm_g, N_KV_HEADS).reshape(1, KV_WIDTH)
    cog = conv_out_g.reshape(1, CONV_WIDTH)
    gmat = jnp.kron(jnp.eye(CONV_GROUPS, dtype=F32), jnp.full((HEAD_DIM, HEAD_DIM), 1.0 / HEAD_DIM, F32))
    cos_t, sin_t = _rope_tables(n)

    u_meta, k_meta, v_meta = pl.pallas_call(
        _meta_proj_kernel,
        out_shape=(jax.ShapeDtypeStruct((N_META, CONV_WIDTH), F32),
                   jax.ShapeDtypeStruct((N_KV_HEADS, N_META, HEAD_DIM), BF16),
                   jax.ShapeDtypeStruct((N_KV_HEADS, N_META, HEAD_DIM), BF16)),
        compiler_params=cparams(),
        name="meta_proj",
    )(meta_tokens, g_mix, w_in_b, kg, gmat[0:KV_WIDTH, 0:KV_WIDTH])
    pad = ((0, 0), (0, META_PAD - N_META), (0, 0))
    k_mp = jnp.pad(k_meta, pad)
    v_mp = jnp.concatenate([jnp.pad(v_meta, pad), jnp.ones((N_KV_HEADS, META_PAD, HEAD_DIM), BF16)], axis=-1)
    meta_bias = jnp.where(jnp.arange(META_PAD) < N_META, 0.0, NEG_BIG).astype(F32).reshape(1, META_PAD)

    const2 = lambda i, j: (0, 0)
    yconv, q, k, v = pl.pallas_call(
        functools.partial(_in_proj_kernel, n=n, rc=rc, bp=bp),
        grid=(b, n // rc),
        in_specs=[
            pl.BlockSpec((1, rc, D_MODEL), lambda i, j: (jnp.minimum(i, bp - 1), j, 0)),
            pl.BlockSpec((1, rc, D_MODEL), lambda i, j: (jnp.maximum(i - bp, 0), j, 0)),
            pl.BlockSpec((1, D_MODEL), const2),
            pl.BlockSpec((D_MODEL, IN_WIDTH), const2),
            pl.BlockSpec((3, CONV_WIDTH), const2),
            pl.BlockSpec((1, ATTN_WIDTH), const2),
            pl.BlockSpec((1, KV_WIDTH), const2),
            pl.BlockSpec((1, CONV_WIDTH), const2),
            pl.BlockSpec((rc, LANES), lambda i, j: (j, 0)),
            pl.BlockSpec((rc, LANES), lambda i, j: (j, 0)),
            pl.BlockSpec((1, CONV_WIDTH), const2),
            pl.BlockSpec((CONV_WIDTH, CONV_WIDTH), const2),
        ],
        out_specs=(
            pl.BlockSpec((1, n, CONV_WIDTH), lambda i, j: (i, 0, 0)),
            pl.BlockSpec((1, N_HEADS, rc, HEAD_DIM), lambda i, j: (i, 0, j, 0)),
            pl.BlockSpec((1, N_KV_HEADS, rc, HEAD_DIM), lambda i, j: (i, 0, j, 0)),
            pl.BlockSpec((1, N_KV_HEADS, rc, 2 * HEAD_DIM), lambda i, j: (i, 0, j, 0)),
        ),
        out_shape=(
            jax.ShapeDtypeStruct((b, n, CONV_WIDTH), BF16),
            jax.ShapeDtypeStruct((b, N_HEADS, n, HEAD_DIM), BF16),
            jax.ShapeDtypeStruct((b, N_KV_HEADS, n, HEAD_DIM), BF16),
            jax.ShapeDtypeStruct((b, N_KV_HEADS, n, 2 * HEAD_DIM), BF16),
        ),
        scratch_shapes=[pltpu.VMEM((n + 2 * SUBLANES, CONV_WIDTH), F32), pltpu.VMEM((n, CONV_WIDTH), F32)],
        compiler_params=cparams(dimension_semantics=("parallel", "arbitrary")),
        name="in_proj",
    )(x_prompt, x_sample, g_mix, w_in_b, conv_w, qg, kg, cog, cos_t, sin_t, u_meta[N_META - 1:N_META], gmat)

    h = pl.pallas_call(
        functools.partial(_attn_out_kernel, tq=tq, bp=bp),
        grid=(b, n // tq),
        in_specs=[
            pl.BlockSpec((1, tq, D_MODEL), lambda i, j: (jnp.minimum(i, bp - 1), j, 0)),
            pl.BlockSpec((1, tq, D_MODEL), lambda i, j: (jnp.maximum(i - bp, 0), j, 0)),
            pl.BlockSpec((1, tq, CONV_WIDTH), lambda i, j: (i, j, 0)),
            pl.BlockSpec((1, N_HEADS, tq, HEAD_DIM), lambda i, j: (i, 0, j, 0)),
            pl.BlockSpec((1, N_KV_HEADS, n, HEAD_DIM), lambda i, j: (i, 0, 0, 0)),
            pl.BlockSpec((1, N_KV_HEADS, n, 2 * HEAD_DIM), lambda i, j: (i, 0, 0, 0)),
            pl.BlockSpec((N_KV_HEADS, META_PAD, HEAD_DIM), lambda i, j: (0, 0, 0)),
            pl.BlockSpec((N_KV_HEADS, META_PAD, 2 * HEAD_DIM), lambda i, j: (0, 0, 0)),
            pl.BlockSpec((1, META_PAD), const2),
            pl.BlockSpec((N_HEADS, HEAD_DIM), const2),
            pl.BlockSpec((CONV_WIDTH + ATTN_WIDTH, D_MODEL), const2),
        ],
        out_specs=pl.BlockSpec((1, tq, D_MODEL), lambda i, j: (i, j, 0)),
        out_shape=jax.ShapeDtypeStruct((b, n, D_MODEL), F32),
        compiler_params=cparams(dimension_semantics=("parallel", "parallel")),
        name="attn_out",
    )(x_prompt, x_sample, yconv, q, k, v, k_mp, v_mp, meta_bias, attn_out_g.reshape(N_HEADS, HEAD_DIM), w_out_b)

    wq_t = peer_wq.T.astype(BF16)
    sk = peer_subkeys.reshape(2 * PEER_HEADS, PEER_NKEYS, PEER_HALF).astype(BF16)
    u_b = peer_u.astype(BF16)
    v_t = peer_v.T.astype(BF16)
    nc = tt // LANES
    n_eb = PEER_EXPERTS // et
    sel_f32 = pltpu.VMEM((PEER_HEADS, nc, PEER_NKEYS, LANES), F32)
    sel_b16 = pltpu.VMEM((PEER_HEADS, nc, PEER_NKEYS // PACK, PACK, LANES), BF16)
    tp = bp * n // tt
    out_p, out_s = pl.pallas_call(
        functools.partial(_peer_kernel, tt=tt, et=et, tp=tp),
        grid=(t // tt, n_eb),
        in_specs=[
            pl.BlockSpec((tt, D_MODEL), lambda i, j: (i, 0)),
            pl.BlockSpec((1, D_MODEL), const2),
            pl.BlockSpec((PEER_HEADS * PEER_QDIM, D_MODEL), const2),
            pl.BlockSpec((2 * PEER_HEADS, PEER_NKEYS, PEER_HALF), lambda i, j: (0, 0, 0)),
            pl.BlockSpec((et, D_MODEL), lambda i, j: (j, 0)),
            pl.BlockSpec((D_MODEL, et), lambda i, j: (0, jnp.maximum(j - 1, 0))),
            pl.BlockSpec((D_MODEL, et), lambda i, j: (0, n_eb - 1)),
        ],
        out_specs=(pl.BlockSpec((tt, D_MODEL), lambda i, j: (jnp.minimum(i, tp - 1), 0)),
                   pl.BlockSpec((tt, D_MODEL), lambda i, j: (jnp.maximum(i - tp, 0), 0))),
        out_shape=(jax.ShapeDtypeStruct((bp * n, D_MODEL), F32),
                   jax.ShapeDtypeStruct((t - bp * n, D_MODEL), F32)),
        scratch_shapes=[
            pltpu.VMEM((D_MODEL, tt), BF16),
            pltpu.VMEM((2 * PEER_HEADS, nc, PEER_NKEYS, LANES), F32),
            sel_b16, sel_b16, sel_f32, sel_f32,
            pltpu.VMEM((D_MODEL, tt), F32),
            pltpu.VMEM((et, tt), BF16),
            pltpu.VMEM((et, tt), BF16),
            pltpu.VMEM((et, tt), BF16),
        ],
        compiler_params=cparams(dimension_semantics=("arbitrary", "arbitrary")),
        name="peer",
    )(h.reshape(t, D_MODEL), norm_ffn_g.reshape(1, D_MODEL), wq_t, sk, u_b, v_t, v_t)
    return out_p.reshape(bp, n, D_MODEL), out_s.reshape(b - bp, n, D_MODEL)


def kernel(x_prompt, x_sample, meta_tokens, norm_mix_g, w_in, conv_w, q_norm_g, k_norm_g, conv_out_g,
           attn_out_g, w_out, norm_ffn_g, peer_wq, peer_subkeys, peer_u, peer_v):
    assert x_prompt.shape[1:] == x_sample.shape[1:]
    return _encode_all(x_prompt, x_sample, meta_tokens, norm_mix_g[0], w_in[0], conv_w[0], q_norm_g[0],
                       k_norm_g[0], conv_out_g[0], attn_out_g[0], w_out[0], norm_ffn_g[0], peer_wq[0],
                       peer_subkeys[0], peer_u[0], peer_v[0])
```

```python
import functools

import jax
import jax.numpy as jnp
from jax import lax
from jax.experimental import pallas as pl
from jax.experimental.pallas import tpu as pltpu

D_MODEL = 1024
N_META = 16
GRID_W = 64
CONV_WIDTH = 512
CONV_GROUPS = 8
N_HEADS = 8
N_KV_HEADS = 2
Q_PER_KV = N_HEADS // N_KV_HEADS
HEAD_DIM = 64
AXIS_DIM = HEAD_DIM // 2
ROPE_THETA = 10000.0
ATTN_WIDTH = N_HEADS * HEAD_DIM
KV_WIDTH = N_KV_HEADS * HEAD_DIM
IN_WIDTH = 3 * CONV_WIDTH + ATTN_WIDTH + 2 * KV_WIDTH
PEER_HEADS = 8
PEER_NKEYS = 128
PEER_EXPERTS = PEER_NKEYS * PEER_NKEYS
PEER_QDIM = 256
PEER_HALF = PEER_QDIM // 2
PEER_TOPK = 16
EPS = 1e-6

LANES = 128
SUBLANES = 8
META_PAD = LANES
NEG_BIG = -1e30
NOT_RANKED = 99.0
VMEM_LIMIT = 56 * 1024 * 1024

F32 = jnp.float32
BF16 = jnp.bfloat16
HI = lax.Precision.HIGHEST


def _rms_rows(x, g):
    return x * lax.rsqrt(jnp.mean(x * x, axis=-1, keepdims=True) + EPS) * g


def _group_rms(x, gmat, g):
    ms = jnp.dot(x * x, gmat, precision=HI, preferred_element_type=F32)
    return x * lax.rsqrt(ms + EPS) * g


def _rope(x, cos, sin):
    lane = lax.broadcasted_iota(jnp.int32, x.shape, 1)
    swapped = jnp.where(lane % (2 * (AXIS_DIM // 2)) < AXIS_DIM // 2,
                        pltpu.roll(x, LANES - AXIS_DIM // 2, 1), pltpu.roll(x, AXIS_DIM // 2, 1))
    return x * cos + swapped * sin


def _meta_proj_kernel(m_ref, g_ref, w_ref, kg_ref, gm_ref, u_ref, k_ref, v_ref):
    xn = _rms_rows(m_ref[...], g_ref[...]).astype(BF16)
    z = jnp.dot(xn, w_ref[...], preferred_element_type=F32)
    u_ref[...] = z[:, CONV_WIDTH:2 * CONV_WIDTH] * z[:, 2 * CONV_WIDTH:3 * CONV_WIDTH]
    k0 = 3 * CONV_WIDTH + ATTN_WIDTH
    kn = _group_rms(z[:, k0:k0 + KV_WIDTH], gm_ref[...], kg_ref[...])
    for g in range(N_KV_HEADS):
        k_ref[g] = kn[:, g * HEAD_DIM:(g + 1) * HEAD_DIM].astype(BF16)
        v_ref[g] = z[:, k0 + KV_WIDTH + g * HEAD_DIM:k0 + KV_WIDTH + (g + 1) * HEAD_DIM].astype(BF16)


def _in_proj_kernel(xp_ref, xs_ref, g_ref, w_ref, cw_ref, qg_ref, kg_ref, cog_ref, cos_ref, sin_ref, um_ref,
                    gm_ref, yconv_ref, q_ref, k_ref, v_ref, u_scr, gb_scr, *, n, rc, bp):
    c = pl.program_id(1)
    r0 = pl.multiple_of(c * rc, rc)
    halo = SUBLANES

    @pl.when(c == 0)
    def _():
        u_scr[0:halo, :] = jnp.broadcast_to(um_ref[...], (halo, CONV_WIDTH))
        u_scr[halo + n:2 * halo + n, :] = jnp.zeros((halo, CONV_WIDTH), F32)

    x = jnp.where(pl.program_id(0) < bp, xp_ref[0], xs_ref[0])
    xn = _rms_rows(x, g_ref[...]).astype(BF16)
    z = jnp.dot(xn, w_ref[...], preferred_element_type=F32)
    gb_scr[pl.ds(r0, rc), :] = z[:, 0:CONV_WIDTH]
    u_scr[pl.ds(halo + r0, rc), :] = z[:, CONV_WIDTH:2 * CONV_WIDTH] * z[:, 2 * CONV_WIDTH:3 * CONV_WIDTH]

    cos = cos_ref[...]
    sin = sin_ref[...]
    q0 = 3 * CONV_WIDTH
    qn = _group_rms(z[:, q0:q0 + ATTN_WIDTH], gm_ref[...], qg_ref[...])
    for j in range(ATTN_WIDTH // LANES):
        qr = _rope(qn[:, j * LANES:(j + 1) * LANES], cos, sin) * (HEAD_DIM ** -0.5)
        q_ref[0, 2 * j] = qr[:, :HEAD_DIM].astype(BF16)
        q_ref[0, 2 * j + 1] = qr[:, HEAD_DIM:].astype(BF16)
    k0 = q0 + ATTN_WIDTH
    kr = _rope(_group_rms(z[:, k0:k0 + KV_WIDTH], gm_ref[0:KV_WIDTH, 0:KV_WIDTH], kg_ref[...]), cos, sin)
    for g in range(N_KV_HEADS):
        k_ref[0, g] = kr[:, g * HEAD_DIM:(g + 1) * HEAD_DIM].astype(BF16)
        v_ref[0, g] = z[:, k0 + KV_WIDTH + g * HEAD_DIM:k0 + KV_WIDTH + (g + 1) * HEAD_DIM].astype(BF16)

    @pl.when(c == pl.num_programs(1) - 1)
    def _():
        for j in range(n // rc):
            s = j * rc
            um = u_scr[halo + s - 1:halo + s - 1 + rc, :]
            uc = u_scr[halo + s:halo + s + rc, :]
            up = u_scr[halo + s + 1:halo + s + 1 + rc, :]
            y = gb_scr[s:s + rc, :] * (um * cw_ref[0:1, :] + uc * cw_ref[1:2, :] + up * cw_ref[2:3, :])
            yconv_ref[0, s:s + rc, :] = _group_rms(y, gm_ref[...], cog_ref[...]).astype(BF16)


def _attn_out_kernel(xp_ref, xs_ref, yconv_ref, q_ref, k_ref, v_ref, km_ref, vm_ref, bias_ref, ag_ref, wo_ref,
                     h_ref, *, tq, bp):
    x = jnp.where(pl.program_id(0) < bp, xp_ref[0], xs_ref[0])
    acc = x + jnp.dot(yconv_ref[0], wo_ref[0:CONV_WIDTH, :], preferred_element_type=F32)
    dn = (((1,), (1,)), ((), ()))
    for g in range(N_KV_HEADS):
        qg = q_ref[0, g * Q_PER_KV:(g + 1) * Q_PER_KV].reshape(Q_PER_KV * tq, HEAD_DIM)
        s = lax.dot_general(qg, k_ref[0, g], dn, preferred_element_type=F32)
        sm = lax.dot_general(qg, km_ref[g], dn, preferred_element_type=F32) + bias_ref[...]
        m = jnp.maximum(jnp.max(s, axis=-1, keepdims=True), jnp.max(sm, axis=-1, keepdims=True))
        p = jnp.exp(s - m)
        pm = jnp.exp(sm - m)
        l = jnp.sum(p, axis=-1, keepdims=True) + jnp.sum(pm, axis=-1, keepdims=True)
        o = (jnp.dot(p.astype(BF16), v_ref[0, g], preferred_element_type=F32)
             + jnp.dot(pm.astype(BF16), vm_ref[g], preferred_element_type=F32)) / l
        for r in range(Q_PER_KV):
            head = g * Q_PER_KV + r
            y = _rms_rows(o[r * tq:(r + 1) * tq], ag_ref[head:head + 1, :]).astype(BF16)
            w0 = CONV_WIDTH + head * HEAD_DIM
            acc = acc + jnp.dot(y, wo_ref[w0:w0 + HEAD_DIM, :], preferred_element_type=F32)
    h_ref[0] = acc


PACK = 16
N_CAND = 80


def _top16_ranks(s):
    rows = lax.broadcasted_iota(jnp.int32, s.shape, 0)
    r16 = lax.broadcasted_iota(jnp.int32, (PEER_TOPK, LANES), 0)
    rank = jnp.full(s.shape, NOT_RANKED, F32)
    vals = jnp.zeros((PEER_TOPK, LANES), F32)
    for a in range(PEER_TOPK):
        m = jnp.max(s, axis=0, keepdims=True)
        idx = jnp.min(jnp.where(s == m, rows, PEER_NKEYS), axis=0, keepdims=True)
        sel = rows == idx
        rank = jnp.where(sel, float(a), rank)
        s = jnp.where(sel, -jnp.inf, s)
        vals = jnp.where(r16 == a, m, vals)
    return rank, vals


def _batcher_pairs(lo, hi):
    def merge(lo, hi, r):
        step = r * 2
        if step < hi - lo:
            yield from merge(lo, hi, step)
            yield from merge(lo + r, hi, step)
            yield from [(i, i + r) for i in range(lo + r, hi - r, step)]
        else:
            yield (lo, lo + r)
    if hi - lo >= 1:
        mid = lo + (hi - lo) // 2
        yield from _batcher_pairs(lo, mid)
        yield from _batcher_pairs(mid + 1, hi)
        yield from merge(lo, hi, 1)


def _compare_exchange(t, pairs):
    for i, j in pairs:
        t[i], t[j] = jnp.maximum(t[i], t[j]), jnp.minimum(t[i], t[j])


def _top16_values(s):
    k = PEER_TOPK
    t = [s[SUBLANES * i:SUBLANES * (i + 1)] for i in range(PEER_NKEYS // SUBLANES)]
    _compare_exchange(t, list(_batcher_pairs(0, k - 1)))
    bitonic = [(i, i + d) for d in (8, 4, 2, 1) for i in range(k) if not i & d]
    for shift in (1, 2, 4):
        t = [jnp.maximum(t[i], pltpu.roll(t[k - 1 - i], shift, 0)) for i in range(k)]
        _compare_exchange(t, bitonic)
    r16 = lax.broadcasted_iota(jnp.int32, (k, LANES), 0)
    vals = jnp.concatenate([t[0], t[0]], axis=0)
    for a in range(1, k):
        vals = jnp.where(r16 == a, jnp.concatenate([t[a], t[a]], axis=0), vals)
    return vals


def _pair_candidates(v1, v2):
    r8 = lax.broadcasted_iota(jnp.int32, (SUBLANES, LANES), 0)

    def bc(row):
        return jnp.broadcast_to(row, (SUBLANES, LANES))

    cands = [bc(v1[0:1]) + v2[0:8], bc(v1[0:1]) + v2[8:16]]
    poss = [r8, r8 + 8]
    for a in range(1, 8):
        cands.append(jnp.where(r8 < PEER_TOPK // (a + 1), bc(v1[a:a + 1]) + v2[0:8], -jnp.inf))
        poss.append(r8 + PEER_TOPK * a)
    cands.append(v1[8:16] + bc(v2[0:1]))
    poss.append((r8 + 8) * PEER_TOPK)
    return jnp.concatenate(cands, axis=0), jnp.concatenate(poss, axis=0)


def _pair_counts(cand0, sel):
    r8 = lax.broadcasted_iota(jnp.int32, (SUBLANES, LANES), 0)
    self32 = sel.astype(F32)
    z = jnp.sum(jnp.where(sel, jnp.exp(cand0 - cand0[0:1]), 0.0), axis=0, keepdims=True)
    n_lo = jnp.broadcast_to(jnp.sum(self32[0:16], axis=0, keepdims=True), (SUBLANES, LANES))
    for a in range(1, 8):
        na = jnp.sum(self32[8 + 8 * a:16 + 8 * a], axis=0, keepdims=True)
        n_lo = jnp.where(r8 == a, na, n_lo)
    return jnp.concatenate([n_lo, self32[N_CAND - 8:N_CAND]], axis=0), z


def _select_exact(s1, s2):
    rk1, v1 = _top16_ranks(s1)
    rk2, v2 = _top16_ranks(s2)
    cand0, pos = _pair_candidates(v1, v2)
    cand = cand0
    sel = jnp.zeros(cand.shape, jnp.bool_)
    for _ in range(PEER_TOPK):
        m = jnp.max(cand, axis=0, keepdims=True)
        first = jnp.min(jnp.where(cand == m, pos, PEER_TOPK * PEER_TOPK), axis=0, keepdims=True)
        hit = pos == first
        sel = jnp.logical_or(sel, hit)
        cand = jnp.where(hit, -jnp.inf, cand)
    counts, z = _pair_counts(cand0, sel)
    n_dense = jnp.zeros(s1.shape, F32)
    for a in range(PEER_TOPK):
        n_dense = jnp.where(rk1 == float(a), counts[a:a + 1], n_dense)
    return rk2, n_dense, z, v1[0:1], v2[0:1]


def _select_distinct(s1, s2):
    v1 = _top16_values(s1)
    v2 = _top16_values(s2)
    rk2 = jnp.zeros(s2.shape, F32)
    for b in range(PEER_TOPK):
        rk2 = jnp.where(s2 < v2[b:b + 1], float(b + 1), rk2)
    cand0, _ = _pair_candidates(v1, v2)
    m = cand0[0:1]
    for _ in range(1, PEER_TOPK):
        m = jnp.max(jnp.where(cand0 < m, cand0, -jnp.inf), axis=0, keepdims=True)
    sel = cand0 >= m
    counts, z = _pair_counts(cand0, sel)
    n_dense = jnp.zeros(s1.shape, F32)
    for a in range(PEER_TOPK):
        n_dense = jnp.where(s1 == v1[a:a + 1], counts[a:a + 1], n_dense)
    k = float(PEER_TOPK)
    n1 = jnp.sum((s1 >= v1[PEER_TOPK - 1:PEER_TOPK]).astype(F32), axis=0, keepdims=True)
    n2 = jnp.sum((rk2 < k).astype(F32), axis=0, keepdims=True)
    n3 = jnp.sum(sel.astype(F32), axis=0, keepdims=True)
    bad = jnp.logical_or(jnp.logical_or(n1 != k, n2 != k), n3 != k)
    dup = jnp.logical_or(v1[0:PEER_TOPK - 1] == v1[1:PEER_TOPK], v2[0:PEER_TOPK - 1] == v2[1:PEER_TOPK])
    return (rk2, n_dense, z, v1[0:1], v2[0:1]), jnp.logical_not(jnp.logical_or(jnp.any(bad), jnp.any(dup)))


def _peer_kernel(h_ref, g_ref, wqt_ref, sk_ref, u_ref, vta_ref, vtb_ref, vtl_ref, outp_ref, outs_ref,
                 xnt_scr, sc_scr, rk2_scr, e2_scr, n_scr, r_scr, acc_scr, pa_scr, pb_scr, *, tt, et, tp):
    j = pl.program_id(1)
    last = pl.num_programs(1) - 1
    nc = tt // LANES
    eh = et // 2
    n_ih = eh // PEER_NKEYS
    n_pk = PEER_NKEYS // PACK

    @pl.when(j == 0)
    def _prologue():
        xn = _rms_rows(h_ref[...], g_ref[...])
        xnt = xn.T.astype(BF16)
        xnt_scr[...] = xnt
        qt = jnp.dot(wqt_ref[...], xnt, preferred_element_type=F32).astype(BF16)
        for hp in range(2 * PEER_HEADS):
            st = jnp.dot(sk_ref[hp], qt[hp * PEER_HALF:(hp + 1) * PEER_HALF, :],
                         preferred_element_type=F32)
            for c in range(nc):
                sc_scr[hp, c] = st[:, c * LANES:(c + 1) * LANES]

        def head_body(hh, carry):
            def chunk_body(c, carry2):
                s1 = sc_scr[2 * hh, c]
                s2 = sc_scr[2 * hh + 1, c]

                def publish(rk2, n_dense, z, m1, m2):
                    rk2_scr[hh, c] = rk2.astype(BF16).reshape(n_pk, PACK, LANES)
                    e2_scr[hh, c] = jnp.exp(s2 - m2).astype(BF16).reshape(n_pk, PACK, LANES)
                    n_scr[hh, c] = n_dense
                    r_scr[hh, c] = jnp.where(n_dense > 0.0, jnp.exp(s1 - m1) / z, 0.0)

                picked, distinct = _select_distinct(s1, s2)
                publish(*picked)

                @pl.when(jnp.logical_not(distinct))
                def _():
                    publish(*_select_exact(s1, s2))
                return carry2
            return lax.fori_loop(0, nc, chunk_body, carry)
        lax.fori_loop(0, PEER_HEADS, head_body, 0)
        acc_scr[...] = jnp.zeros(acc_scr.shape, F32)
        pb_scr[...] = jnp.zeros((eh, tt), BF16)

    def build(half, p_ref):
        a_t = jnp.dot(u_ref[half * eh:(half + 1) * eh, :], xnt_scr[...], preferred_element_type=F32)
        hid = (0.5 * a_t * (1.0 + lax.erf(a_t * (2.0 ** -0.5)))).astype(BF16)
        for ii in range(n_ih):
            i = (2 * j + half) * n_ih + ii
            for c in range(nc):
                w = jnp.zeros((n_pk, PACK, LANES), BF16)
                for hh in range(PEER_HEADS):
                    n_row = jnp.broadcast_to(n_scr[hh, c, pl.ds(i, 1), :], (PACK, LANES)).astype(BF16)
                    r_row = jnp.broadcast_to(r_scr[hh, c, pl.ds(i, 1), :], (PACK, LANES)).astype(BF16)
                    w = w + jnp.where(rk2_scr[hh, c] < n_row, e2_scr[hh, c], jnp.zeros((), BF16)) * r_row
                rows = slice(ii * PEER_NKEYS, (ii + 1) * PEER_NKEYS)
                cols = slice(c * LANES, (c + 1) * LANES)
                p_ref[rows, cols] = hid[rows, cols] * w.reshape(PEER_NKEYS, LANES)

    build(0, pa_scr)
    acc_scr[...] += jnp.dot(vtb_ref[...], pb_scr[...], preferred_element_type=F32)
    build(1, pb_scr)
    acc_scr[...] += jnp.dot(vta_ref[...], pa_scr[...], preferred_element_type=F32)

    @pl.when(j == last)
    def _epilogue():
        tail = jnp.dot(vtl_ref[...], pb_scr[...], preferred_element_type=F32)
        res = h_ref[...] + (acc_scr[...] + tail).T
        i = pl.program_id(0)

        @pl.when(i < tp)
        def _():
            outp_ref[...] = res

        @pl.when(i >= tp)
        def _():
            outs_ref[...] = res


def _rope_tables(n):
    rows = n // GRID_W
    row = jnp.repeat(jnp.arange(rows, dtype=jnp.int32), GRID_W).astype(F32)
    col = jnp.tile(jnp.arange(GRID_W, dtype=jnp.int32), rows).astype(F32)
    freqs = ROPE_THETA ** (-jnp.arange(0, AXIS_DIM, 2, dtype=F32) / AXIS_DIM)
    ang_r = row[:, None] * freqs[None, :]
    ang_c = col[:, None] * freqs[None, :]
    cos_h = jnp.concatenate([jnp.cos(ang_r)] * 2 + [jnp.cos(ang_c)] * 2, axis=-1)
    sin_h = jnp.concatenate([-jnp.sin(ang_r), jnp.sin(ang_r), -jnp.sin(ang_c), jnp.sin(ang_c)], axis=-1)
    return jnp.tile(cos_h, (1, LANES // HEAD_DIM)), jnp.tile(sin_h, (1, LANES // HEAD_DIM))


def _tile_sizes(n, t_prompt, t_sample):
    rc = min(512, n)
    tq = min(128, n)
    tt = min(512, t_prompt, t_sample)
    et = 1024
    assert n % rc == 0 and n % tq == 0 and n % GRID_W == 0 and tt % LANES == 0
    assert t_prompt % tt == 0 and t_sample % tt == 0
    return rc, tq, tt, et


def _encode_all(x_prompt, x_sample, meta_tokens, norm_mix_g, w_in, conv_w, q_norm_g, k_norm_g, conv_out_g, attn_out_g,
                w_out, norm_ffn_g, peer_wq, peer_subkeys, peer_u, peer_v):
    bp, n, _ = x_prompt.shape
    b = bp + x_sample.shape[0]
    t = b * n
    rc, tq, tt, et = _tile_sizes(n, bp * n, t - bp * n)
    cparams = functools.partial(pltpu.CompilerParams, vmem_limit_bytes=VMEM_LIMIT)

    w_in_b = w_in.astype(BF16)
    w_out_b = w_out.astype(BF16)
    g_mix = norm_mix_g.reshape(1, D_MODEL)
    qg = jnp.tile(q_norm_g, N_HEADS).reshape(1, ATTN_WIDTH)
    kg = jnp.tile(k_norm_g, N_KV_HEADS).reshape(1, KV_WIDTH)
    cog = conv_out_g.reshape(1, CONV_WIDTH)
    gmat = jnp.kron(jnp.eye(CONV_GROUPS, dtype=F32), jnp.full((HEAD_DIM, HEAD_DIM), 1.0 / HEAD_DIM, F32))
    cos_t, sin_t = _rope_tables(n)

    u_meta, k_meta, v_meta = pl.pallas_call(
        _meta_proj_kernel,
        out_shape=(jax.ShapeDtypeStruct((N_META, CONV_WIDTH), F32),
                   jax.ShapeDtypeStruct((N_KV_HEADS, N_META, HEAD_DIM), BF16),
                   jax.ShapeDtypeStruct((N_KV_HEADS, N_META, HEAD_DIM), BF16)),
        compiler_params=cparams(),
        name="meta_proj",
    )(meta_tokens, g_mix, w_in_b, kg, gmat[0:KV_WIDTH, 0:KV_WIDTH])
    pad = ((0, 0), (0, META_PAD - N_META), (0, 0))
    k_mp = jnp.pad(k_meta, pad)
    v_mp = jnp.pad(v_meta, pad)
    meta_bias = jnp.where(jnp.arange(META_PAD) < N_META, 0.0, NEG_BIG).astype(F32).reshape(1, META_PAD)

    const2 = lambda i, j: (0, 0)
    yconv, q, k, v = pl.pallas_call(
        functools.partial(_in_proj_kernel, n=n, rc=rc, bp=bp),
        grid=(b, n // rc),
        in_specs=[
            pl.BlockSpec((1, rc, D_MODEL), lambda i, j: (jnp.minimum(i, bp - 1), j, 0)),
            pl.BlockSpec((1, rc, D_MODEL), lambda i, j: (jnp.maximum(i - bp, 0), j, 0)),
            pl.BlockSpec((1, D_MODEL), const2),
            pl.BlockSpec((D_MODEL, IN_WIDTH), const2),
            pl.BlockSpec((3, CONV_WIDTH), const2),
            pl.BlockSpec((1, ATTN_WIDTH), const2),
            pl.BlockSpec((1, KV_WIDTH), const2),
            pl.BlockSpec((1, CONV_WIDTH), const2),
            pl.BlockSpec((rc, LANES), lambda i, j: (j, 0)),
            pl.BlockSpec((rc, LANES), lambda i, j: (j, 0)),
            pl.BlockSpec((1, CONV_WIDTH), const2),
            pl.BlockSpec((CONV_WIDTH, CONV_WIDTH), const2),
        ],
        out_specs=(
            pl.BlockSpec((1, n, CONV_WIDTH), lambda i, j: (i, 0, 0)),
            pl.BlockSpec((1, N_HEADS, rc, HEAD_DIM), lambda i, j: (i, 0, j, 0)),
            pl.BlockSpec((1, N_KV_HEADS, rc, HEAD_DIM), lambda i, j: (i, 0, j, 0)),
            pl.BlockSpec((1, N_KV_HEADS, rc, HEAD_DIM), lambda i, j: (i, 0, j, 0)),
        ),
        out_shape=(
            jax.ShapeDtypeStruct((b, n, CONV_WIDTH), BF16),
            jax.ShapeDtypeStruct((b, N_HEADS, n, HEAD_DIM), BF16),
            jax.ShapeDtypeStruct((b, N_KV_HEADS, n, HEAD_DIM), BF16),
            jax.ShapeDtypeStruct((b, N_KV_HEADS, n, HEAD_DIM), BF16),
        ),
        scratch_shapes=[pltpu.VMEM((n + 2 * SUBLANES, CONV_WIDTH), F32), pltpu.VMEM((n, CONV_WIDTH), F32)],
        compiler_params=cparams(dimension_semantics=("parallel", "arbitrary")),
        name="in_proj",
    )(x_prompt, x_sample, g_mix, w_in_b, conv_w, qg, kg, cog, cos_t, sin_t, u_meta[N_META - 1:N_META], gmat)

    h = pl.pallas_call(
        functools.partial(_attn_out_kernel, tq=tq, bp=bp),
        grid=(b, n // tq),
        in_specs=[
            pl.BlockSpec((1, tq, D_MODEL), lambda i, j: (jnp.minimum(i, bp - 1), j, 0)),
            pl.BlockSpec((1, tq, D_MODEL), lambda i, j: (jnp.maximum(i - bp, 0), j, 0)),
            pl.BlockSpec((1, tq, CONV_WIDTH), lambda i, j: (i, j, 0)),
            pl.BlockSpec((1, N_HEADS, tq, HEAD_DIM), lambda i, j: (i, 0, j, 0)),
            pl.BlockSpec((1, N_KV_HEADS, n, HEAD_DIM), lambda i, j: (i, 0, 0, 0)),
            pl.BlockSpec((1, N_KV_HEADS, n, HEAD_DIM), lambda i, j: (i, 0, 0, 0)),
            pl.BlockSpec((N_KV_HEADS, META_PAD, HEAD_DIM), lambda i, j: (0, 0, 0)),
            pl.BlockSpec((N_KV_HEADS, META_PAD, HEAD_DIM), lambda i, j: (0, 0, 0)),
            pl.BlockSpec((1, META_PAD), const2),
            pl.BlockSpec((N_HEADS, HEAD_DIM), const2),
            pl.BlockSpec((CONV_WIDTH + ATTN_WIDTH, D_MODEL), const2),
        ],
        out_specs=pl.BlockSpec((1, tq, D_MODEL), lambda i, j: (i, j, 0)),
        out_shape=jax.ShapeDtypeStruct((b, n, D_MODEL), F32),
        compiler_params=cparams(dimension_semantics=("parallel", "parallel")),
        name="attn_out",
    )(x_prompt, x_sample, yconv, q, k, v, k_mp, v_mp, meta_bias, attn_out_g.reshape(N_HEADS, HEAD_DIM), w_out_b)

    wq_t = peer_wq.T.astype(BF16)
    sk = peer_subkeys.reshape(2 * PEER_HEADS, PEER_NKEYS, PEER_HALF).astype(BF16)
    u_b = peer_u.astype(BF16)
    v_t = peer_v.T.astype(BF16)
    nc = tt // LANES
    n_eb = PEER_EXPERTS // et
    sel_f32 = pltpu.VMEM((PEER_HEADS, nc, PEER_NKEYS, LANES), F32)
    sel_b16 = pltpu.VMEM((PEER_HEADS, nc, PEER_NKEYS // PACK, PACK, LANES), BF16)
    tp = bp * n // tt
    out_p, out_s = pl.pallas_call(
        functools.partial(_peer_kernel, tt=tt, et=et, tp=tp),
        grid=(t // tt, n_eb),
        in_specs=[
            pl.BlockSpec((tt, D_MODEL), lambda i, j: (i, 0)),
            pl.BlockSpec((1, D_MODEL), const2),
            pl.BlockSpec((PEER_HEADS * PEER_QDIM, D_MODEL), const2),
            pl.BlockSpec((2 * PEER_HEADS, PEER_NKEYS, PEER_HALF), lambda i, j: (0, 0, 0)),
            pl.BlockSpec((et, D_MODEL), lambda i, j: (j, 0)),
            pl.BlockSpec((D_MODEL, et // 2), lambda i, j: (0, 2 * j)),
            pl.BlockSpec((D_MODEL, et // 2), lambda i, j: (0, jnp.maximum(2 * j - 1, 0))),
            pl.BlockSpec((D_MODEL, et // 2), lambda i, j: (0, 2 * n_eb - 1)),
        ],
        out_specs=(pl.BlockSpec((tt, D_MODEL), lambda i, j: (jnp.minimum(i, tp - 1), 0)),
                   pl.BlockSpec((tt, D_MODEL), lambda i, j: (jnp.maximum(i - tp, 0), 0))),
        out_shape=(jax.ShapeDtypeStruct((bp * n, D_MODEL), F32),
                   jax.ShapeDtypeStruct((t - bp * n, D_MODEL), F32)),
        scratch_shapes=[
            pltpu.VMEM((D_MODEL, tt), BF16),
            pltpu.VMEM((2 * PEER_HEADS, nc, PEER_NKEYS, LANES), F32),
            sel_b16, sel_b16, sel_f32, sel_f32,
            pltpu.VMEM((D_MODEL, tt), F32),
            pltpu.VMEM((et // 2, tt), BF16),
            pltpu.VMEM((et // 2, tt), BF16),
        ],
        compiler_params=cparams(dimension_semantics=("arbitrary", "arbitrary")),
        name="peer",
    )(h.reshape(t, D_MODEL), norm_ffn_g.reshape(1, D_MODEL), wq_t, sk, u_b, v_t, v_t, v_t)
    return out_p.reshape(bp, n, D_MODEL), out_s.reshape(b - bp, n, D_MODEL)


def kernel(x_prompt, x_sample, meta_tokens, norm_mix_g, w_in, conv_w, q_norm_g, k_norm_g, conv_out_g,
           attn_out_g, w_out, norm_ffn_g, peer_wq, peer_subkeys, peer_u, peer_v):
    assert x_prompt.shape[1:] == x_sample.shape[1:]
    return _encode_all(x_prompt, x_sample, meta_tokens, norm_mix_g[0], w_in[0], conv_w[0], q_norm_g[0],
                       k_norm_g[0], conv_out_g[0], attn_out_g[0], w_out[0], norm_ffn_g[0], peer_wq[0],
                       peer_subkeys[0], peer_u[0], peer_v[0])
```

```python
import functools

import jax
import jax.numpy as jnp
from jax import lax
from jax.experimental import pallas as pl
from jax.experimental.pallas import tpu as pltpu

D_MODEL = 1024
N_META = 16
GRID_W = 64
CONV_WIDTH = 512
CONV_GROUPS = 8
N_HEADS = 8
N_KV_HEADS = 2
Q_PER_KV = N_HEADS // N_KV_HEADS
HEAD_DIM = 64
AXIS_DIM = HEAD_DIM // 2
ROPE_THETA = 10000.0
ATTN_WIDTH = N_HEADS * HEAD_DIM
KV_WIDTH = N_KV_HEADS * HEAD_DIM
IN_WIDTH = 3 * CONV_WIDTH + ATTN_WIDTH + 2 * KV_WIDTH
PEER_HEADS = 8
PEER_NKEYS = 128
PEER_EXPERTS = PEER_NKEYS * PEER_NKEYS
PEER_QDIM = 256
PEER_HALF = PEER_QDIM // 2
PEER_TOPK = 16
EPS = 1e-6

LANES = 128
SUBLANES = 8
META_PAD = LANES
NEG_BIG = -1e30
LOG2E = 1.4426950408889634
NOT_RANKED = 99.0
VMEM_LIMIT = 56 * 1024 * 1024

F32 = jnp.float32
BF16 = jnp.bfloat16


def _rms_rows(x, g):
    return x * lax.rsqrt(jnp.mean(x * x, axis=-1, keepdims=True) + EPS) * g


def _group_rms(x, gmat, g):
    sq = x * x
    hi = sq.astype(BF16)
    lo = (sq - hi.astype(F32)).astype(BF16)
    ss = jnp.dot(hi, gmat, preferred_element_type=F32) + jnp.dot(lo, gmat, preferred_element_type=F32)
    return x * lax.rsqrt(ss * (1.0 / HEAD_DIM) + EPS) * g


def _rope(x, cos, sin):
    lane = lax.broadcasted_iota(jnp.int32, x.shape, 1)
    swapped = jnp.where(lane % (2 * (AXIS_DIM // 2)) < AXIS_DIM // 2,
                        pltpu.roll(x, LANES - AXIS_DIM // 2, 1), pltpu.roll(x, AXIS_DIM // 2, 1))
    return x * cos + swapped * sin


def _meta_proj_kernel(m_ref, g_ref, w_ref, kg_ref, gm_ref, u_ref, k_ref, v_ref):
    xn = _rms_rows(m_ref[...], g_ref[...]).astype(BF16)
    z = jnp.dot(xn, w_ref[...], preferred_element_type=F32)
    u_ref[...] = z[:, CONV_WIDTH:2 * CONV_WIDTH] * z[:, 2 * CONV_WIDTH:3 * CONV_WIDTH]
    k0 = 3 * CONV_WIDTH + ATTN_WIDTH
    kn = _group_rms(z[:, k0:k0 + KV_WIDTH], gm_ref[...], kg_ref[...])
    for g in range(N_KV_HEADS):
        k_ref[g] = kn[:, g * HEAD_DIM:(g + 1) * HEAD_DIM].astype(BF16)
        v_ref[g] = z[:, k0 + KV_WIDTH + g * HEAD_DIM:k0 + KV_WIDTH + (g + 1) * HEAD_DIM].astype(BF16)


def _in_proj_kernel(xp_ref, xs_ref, g_ref, w_ref, cw_ref, qg_ref, kg_ref, cog_ref, cos_ref, sin_ref, um_ref,
                    gm_ref, yconv_ref, q_ref, k_ref, v_ref, u_scr, gb_scr, *, n, rc, bp):
    c = pl.program_id(1)
    r0 = pl.multiple_of(c * rc, rc)
    halo = SUBLANES

    @pl.when(c == 0)
    def _():
        u_scr[0:halo, :] = jnp.broadcast_to(um_ref[...], (halo, CONV_WIDTH))
        u_scr[halo + n:2 * halo + n, :] = jnp.zeros((halo, CONV_WIDTH), F32)

    x = jnp.where(pl.program_id(0) < bp, xp_ref[0], xs_ref[0])
    xn = _rms_rows(x, g_ref[...]).astype(BF16)
    z = jnp.dot(xn, w_ref[...], preferred_element_type=F32)
    gb_scr[pl.ds(r0, rc), :] = z[:, 0:CONV_WIDTH]
    u_scr[pl.ds(halo + r0, rc), :] = z[:, CONV_WIDTH:2 * CONV_WIDTH] * z[:, 2 * CONV_WIDTH:3 * CONV_WIDTH]

    cos = cos_ref[...]
    sin = sin_ref[...]
    q0 = 3 * CONV_WIDTH
    qn = _group_rms(z[:, q0:q0 + ATTN_WIDTH], gm_ref[...], qg_ref[...])
    for j in range(ATTN_WIDTH // LANES):
        qr = _rope(qn[:, j * LANES:(j + 1) * LANES], cos, sin) * (HEAD_DIM ** -0.5 * LOG2E)
        q_ref[0, 2 * j] = qr[:, :HEAD_DIM].astype(BF16)
        q_ref[0, 2 * j + 1] = qr[:, HEAD_DIM:].astype(BF16)
    k0 = q0 + ATTN_WIDTH
    kr = _rope(_group_rms(z[:, k0:k0 + KV_WIDTH], gm_ref[0:KV_WIDTH, 0:KV_WIDTH], kg_ref[...]), cos, sin)
    for g in range(N_KV_HEADS):
        k_ref[0, g] = kr[:, g * HEAD_DIM:(g + 1) * HEAD_DIM].astype(BF16)
        v_ref[0, g, :, 0:HEAD_DIM] = z[:, k0 + KV_WIDTH + g * HEAD_DIM:
                                       k0 + KV_WIDTH + (g + 1) * HEAD_DIM].astype(BF16)
        v_ref[0, g, :, HEAD_DIM:2 * HEAD_DIM] = jnp.ones((rc, HEAD_DIM), BF16)

    @pl.when(c == pl.num_programs(1) - 1)
    def _():
        for j in range(n // rc):
            s = j * rc
            um = u_scr[halo + s - 1:halo + s - 1 + rc, :]
            uc = u_scr[halo + s:halo + s + rc, :]
            up = u_scr[halo + s + 1:halo + s + 1 + rc, :]
            y = gb_scr[s:s + rc, :] * (um * cw_ref[0:1, :] + uc * cw_ref[1:2, :] + up * cw_ref[2:3, :])
            yconv_ref[0, s:s + rc, :] = _group_rms(y, gm_ref[...], cog_ref[...]).astype(BF16)


def _attn_out_kernel(xp_ref, xs_ref, yconv_ref, q_ref, k_ref, v_ref, km_ref, vm_ref, bias_ref, ag_ref, wo_ref,
                     h_ref, y_scr, *, tq, bp, n, kc):
    dn = (((1,), (1,)), ((), ()))
    rows = Q_PER_KV * tq
    qs = [q_ref[0, g * Q_PER_KV:(g + 1) * Q_PER_KV].reshape(rows, HEAD_DIM) for g in range(N_KV_HEADS)]
    state = []
    for g in range(N_KV_HEADS):
        sm = lax.dot_general(qs[g], km_ref[g], dn, preferred_element_type=F32) + bias_ref[...]
        m = jnp.max(sm, axis=-1, keepdims=True)
        p = jnp.exp2(sm - m).astype(BF16)
        state += [m, jnp.dot(p, vm_ref[g], preferred_element_type=F32)]

    def chunk(c, st):
        k0 = pl.multiple_of(c * kc, kc)
        out = []
        for g in range(N_KV_HEADS):
            m, acc = st[2 * g], st[2 * g + 1]
            s = lax.dot_general(qs[g], k_ref[0, g, pl.ds(k0, kc), :], dn, preferred_element_type=F32)
            m_new = jnp.maximum(m, jnp.max(s, axis=-1, keepdims=True))
            p = jnp.exp2(s - m_new).astype(BF16)
            pv = jnp.dot(p, v_ref[0, g, pl.ds(k0, kc), :], preferred_element_type=F32)
            out += [m_new, jnp.exp2(m - m_new) * acc + pv]
        return tuple(out)

    state = lax.fori_loop(0, n // kc, chunk, tuple(state))
    for g in range(N_KV_HEADS):
        acc = state[2 * g + 1]
        o = acc[:, 0:HEAD_DIM] / acc[:, HEAD_DIM:HEAD_DIM + 1]
        for r in range(Q_PER_KV):
            head = g * Q_PER_KV + r
            y = _rms_rows(o[r * tq:(r + 1) * tq], ag_ref[head:head + 1, :])
            y_scr[:, head * HEAD_DIM:(head + 1) * HEAD_DIM] = y.astype(BF16)
    x = jnp.where(pl.program_id(0) < bp, xp_ref[0], xs_ref[0])
    h_ref[0] = (x + jnp.dot(yconv_ref[0], wo_ref[0:CONV_WIDTH, :], preferred_element_type=F32)
                + jnp.dot(y_scr[...], wo_ref[CONV_WIDTH:CONV_WIDTH + ATTN_WIDTH, :],
                          preferred_element_type=F32))


PACK = 16
N_CAND = 80


def _top16_ranks(s):
    rows = lax.broadcasted_iota(jnp.int32, s.shape, 0)
    r16 = lax.broadcasted_iota(jnp.int32, (PEER_TOPK, LANES), 0)
    rank = jnp.full(s.shape, NOT_RANKED, F32)
    vals = jnp.zeros((PEER_TOPK, LANES), F32)
    for a in range(PEER_TOPK):
        m = jnp.max(s, axis=0, keepdims=True)
        idx = jnp.min(jnp.where(s == m, rows, PEER_NKEYS), axis=0, keepdims=True)
        sel = rows == idx
        rank = jnp.where(sel, float(a), rank)
        s = jnp.where(sel, -jnp.inf, s)
        vals = jnp.where(r16 == a, m, vals)
    return rank, vals


def _batcher_pairs(lo, hi):
    def merge(lo, hi, r):
        step = r * 2
        if step < hi - lo:
            yield from merge(lo, hi, step)
            yield from merge(lo + r, hi, step)
            yield from [(i, i + r) for i in range(lo + r, hi - r, step)]
        else:
            yield (lo, lo + r)
    if hi - lo >= 1:
        mid = lo + (hi - lo) // 2
        yield from _batcher_pairs(lo, mid)
        yield from _batcher_pairs(mid + 1, hi)
        yield from merge(lo, hi, 1)


def _compare_exchange(t, pairs):
    for i, j in pairs:
        t[i], t[j] = jnp.maximum(t[i], t[j]), jnp.minimum(t[i], t[j])


def _top16_values(s):
    k = PEER_TOPK
    t = [s[SUBLANES * i:SUBLANES * (i + 1)] for i in range(PEER_NKEYS // SUBLANES)]
    _compare_exchange(t, list(_batcher_pairs(0, k - 1)))
    bitonic = [(i, i + d) for d in (8, 4, 2, 1) for i in range(k) if not i & d]
    for shift in (1, 2, 4):
        t = [jnp.maximum(t[i], pltpu.roll(t[k - 1 - i], shift, 0)) for i in range(k)]
        _compare_exchange(t, bitonic)
    r16 = lax.broadcasted_iota(jnp.int32, (k, LANES), 0)
    vals = jnp.concatenate([t[0], t[0]], axis=0)
    for a in range(1, k):
        vals = jnp.where(r16 == a, jnp.concatenate([t[a], t[a]], axis=0), vals)
    return vals


def _pair_candidates(v1, v2):
    r8 = lax.broadcasted_iota(jnp.int32, (SUBLANES, LANES), 0)

    def bc(row):
        return jnp.broadcast_to(row, (SUBLANES, LANES))

    cands = [bc(v1[0:1]) + v2[0:8], bc(v1[0:1]) + v2[8:16]]
    poss = [r8, r8 + 8]
    for a in range(1, 8):
        cands.append(jnp.where(r8 < PEER_TOPK // (a + 1), bc(v1[a:a + 1]) + v2[0:8], -jnp.inf))
        poss.append(r8 + PEER_TOPK * a)
    cands.append(v1[8:16] + bc(v2[0:1]))
    poss.append((r8 + 8) * PEER_TOPK)
    return jnp.concatenate(cands, axis=0), jnp.concatenate(poss, axis=0)


def _pair_counts(cand0, sel):
    r8 = lax.broadcasted_iota(jnp.int32, (SUBLANES, LANES), 0)
    self32 = sel.astype(F32)
    z = jnp.sum(jnp.where(sel, jnp.exp(cand0 - cand0[0:1]), 0.0), axis=0, keepdims=True)
    n_lo = jnp.broadcast_to(jnp.sum(self32[0:16], axis=0, keepdims=True), (SUBLANES, LANES))
    for a in range(1, 8):
        na = jnp.sum(self32[8 + 8 * a:16 + 8 * a], axis=0, keepdims=True)
        n_lo = jnp.where(r8 == a, na, n_lo)
    return jnp.concatenate([n_lo, self32[N_CAND - 8:N_CAND]], axis=0), z


def _select_exact(s1, s2):
    rk1, v1 = _top16_ranks(s1)
    rk2, v2 = _top16_ranks(s2)
    cand0, pos = _pair_candidates(v1, v2)
    cand = cand0
    sel = jnp.zeros(cand.shape, jnp.bool_)
    for _ in range(PEER_TOPK):
        m = jnp.max(cand, axis=0, keepdims=True)
        first = jnp.min(jnp.where(cand == m, pos, PEER_TOPK * PEER_TOPK), axis=0, keepdims=True)
        hit = pos == first
        sel = jnp.logical_or(sel, hit)
        cand = jnp.where(hit, -jnp.inf, cand)
    counts, z = _pair_counts(cand0, sel)
    n_dense = jnp.zeros(s1.shape, F32)
    for a in range(PEER_TOPK):
        n_dense = jnp.where(rk1 == float(a), counts[a:a + 1], n_dense)
    return rk2, n_dense, z, v1[0:1], v2[0:1]


def _select_distinct(s1, s2):
    v1 = _top16_values(s1)
    v2 = _top16_values(s2)
    rk2 = jnp.zeros(s2.shape, F32)
    for b in range(PEER_TOPK):
        rk2 = jnp.where(s2 < v2[b:b + 1], float(b + 1), rk2)
    cand0, _ = _pair_candidates(v1, v2)
    m = cand0[0:1]
    for _ in range(1, PEER_TOPK):
        m = jnp.max(jnp.where(cand0 < m, cand0, -jnp.inf), axis=0, keepdims=True)
    sel = cand0 >= m
    counts, z = _pair_counts(cand0, sel)
    n_dense = jnp.zeros(s1.shape, F32)
    for a in range(PEER_TOPK):
        n_dense = jnp.where(s1 == v1[a:a + 1], counts[a:a + 1], n_dense)
    k = float(PEER_TOPK)
    n1 = jnp.sum((s1 >= v1[PEER_TOPK - 1:PEER_TOPK]).astype(F32), axis=0, keepdims=True)
    n2 = jnp.sum((rk2 < k).astype(F32), axis=0, keepdims=True)
    n3 = jnp.sum(sel.astype(F32), axis=0, keepdims=True)
    bad = jnp.logical_or(jnp.logical_or(n1 != k, n2 != k), n3 != k)
    dup = jnp.logical_or(v1[0:PEER_TOPK - 1] == v1[1:PEER_TOPK], v2[0:PEER_TOPK - 1] == v2[1:PEER_TOPK])
    return (rk2, n_dense, z, v1[0:1], v2[0:1]), jnp.logical_not(jnp.logical_or(jnp.any(bad), jnp.any(dup)))


def _peer_kernel(h_ref, g_ref, wqt_ref, sk_ref, u_ref, vta_ref, vtb_ref, vtl_ref, outp_ref, outs_ref,
                 xnt_scr, sc_scr, rk2_scr, e2_scr, n_scr, r_scr, acc_scr, pa_scr, pb_scr, *, tt, et, tp):
    j = pl.program_id(1)
    last = pl.num_programs(1) - 1
    nc = tt // LANES
    eh = et // 2
    n_ih = eh // PEER_NKEYS
    n_pk = PEER_NKEYS // PACK

    @pl.when(j == 0)
    def _prologue():
        xn = _rms_rows(h_ref[...], g_ref[...])
        xnt = xn.T.astype(BF16)
        xnt_scr[...] = xnt
        qt = jnp.dot(wqt_ref[...], xnt, preferred_element_type=F32).astype(BF16)
        for hp in range(2 * PEER_HEADS):
            st = jnp.dot(sk_ref[hp], qt[hp * PEER_HALF:(hp + 1) * PEER_HALF, :],
                         preferred_element_type=F32)
            for c in range(nc):
                sc_scr[hp, c] = st[:, c * LANES:(c + 1) * LANES]

        def head_body(hh, carry):
            def chunk_body(c, carry2):
                s1 = sc_scr[2 * hh, c]
                s2 = sc_scr[2 * hh + 1, c]

                def publish(rk2, n_dense, z, m1, m2):
                    rk2_scr[hh, c] = rk2.astype(BF16).reshape(n_pk, PACK, LANES)
                    e2_scr[hh, c] = jnp.exp(s2 - m2).astype(BF16).reshape(n_pk, PACK, LANES)
                    n_scr[hh, c] = n_dense
                    r_scr[hh, c] = jnp.where(n_dense > 0.0, jnp.exp(s1 - m1) / z, 0.0)

                picked, distinct = _select_distinct(s1, s2)
                publish(*picked)

                @pl.when(jnp.logical_not(distinct))
                def _():
                    publish(*_select_exact(s1, s2))
                return carry2
            return lax.fori_loop(0, nc, chunk_body, carry)
        lax.fori_loop(0, PEER_HEADS, head_body, 0)
        acc_scr[...] = jnp.zeros(acc_scr.shape, F32)
        pb_scr[...] = jnp.zeros((eh, tt), BF16)

    def build(half, p_ref):
        a_t = jnp.dot(u_ref[half * eh:(half + 1) * eh, :], xnt_scr[...], preferred_element_type=F32)
        hid = (0.5 * a_t * (1.0 + lax.erf(a_t * (2.0 ** -0.5)))).astype(BF16)
        for ii in range(n_ih):
            i = (2 * j + half) * n_ih + ii
            for c in range(nc):
                w = jnp.zeros((n_pk, PACK, LANES), BF16)
                for hh in range(PEER_HEADS):
                    n_row = jnp.broadcast_to(n_scr[hh, c, pl.ds(i, 1), :], (PACK, LANES)).astype(BF16)
                    r_row = jnp.broadcast_to(r_scr[hh, c, pl.ds(i, 1), :], (PACK, LANES)).astype(BF16)
                    w = w + jnp.where(rk2_scr[hh, c] < n_row, e2_scr[hh, c], jnp.zeros((), BF16)) * r_row
                rows = slice(ii * PEER_NKEYS, (ii + 1) * PEER_NKEYS)
                cols = slice(c * LANES, (c + 1) * LANES)
                p_ref[rows, cols] = hid[rows, cols] * w.reshape(PEER_NKEYS, LANES)

    build(0, pa_scr)
    acc_scr[...] += jnp.dot(vtb_ref[...], pb_scr[...], preferred_element_type=F32)
    build(1, pb_scr)
    acc_scr[...] += jnp.dot(vta_ref[...], pa_scr[...], preferred_element_type=F32)

    @pl.when(j == last)
    def _epilogue():
        tail = jnp.dot(vtl_ref[...], pb_scr[...], preferred_element_type=F32)
        res = h_ref[...] + (acc_scr[...] + tail).T
        i = pl.program_id(0)

        @pl.when(i < tp)
        def _():
            outp_ref[...] = res

        @pl.when(i >= tp)
        def _():
            outs_ref[...] = res


def _rope_tables(n):
    rows = n // GRID_W
    row = jnp.repeat(jnp.arange(rows, dtype=jnp.int32), GRID_W).astype(F32)
    col = jnp.tile(jnp.arange(GRID_W, dtype=jnp.int32), rows).astype(F32)
    freqs = ROPE_THETA ** (-jnp.arange(0, AXIS_DIM, 2, dtype=F32) / AXIS_DIM)
    ang_r = row[:, None] * freqs[None, :]
    ang_c = col[:, None] * freqs[None, :]
    cos_h = jnp.concatenate([jnp.cos(ang_r)] * 2 + [jnp.cos(ang_c)] * 2, axis=-1)
    sin_h = jnp.concatenate([-jnp.sin(ang_r), jnp.sin(ang_r), -jnp.sin(ang_c), jnp.sin(ang_c)], axis=-1)
    return jnp.tile(cos_h, (1, LANES // HEAD_DIM)), jnp.tile(sin_h, (1, LANES // HEAD_DIM))


def _tile_sizes(n, t_prompt, t_sample):
    rc = min(512, n)
    tq = min(256, n)
    tt = min(512, t_prompt, t_sample)
    et = 1024
    assert n % rc == 0 and n % tq == 0 and n % GRID_W == 0 and tt % LANES == 0
    assert t_prompt % tt == 0 and t_sample % tt == 0
    return rc, tq, tt, et


def _encode_all(x_prompt, x_sample, meta_tokens, norm_mix_g, w_in, conv_w, q_norm_g, k_norm_g, conv_out_g, attn_out_g,
                w_out, norm_ffn_g, peer_wq, peer_subkeys, peer_u, peer_v):
    bp, n, _ = x_prompt.shape
    b = bp + x_sample.shape[0]
    t = b * n
    rc, tq, tt, et = _tile_sizes(n, bp * n, t - bp * n)
    cparams = functools.partial(pltpu.CompilerParams, vmem_limit_bytes=VMEM_LIMIT)

    w_in_b = w_in.astype(BF16)
    w_out_b = w_out.astype(BF16)
    g_mix = norm_mix_g.reshape(1, D_MODEL)
    qg = jnp.tile(q_norm_g, N_HEADS).reshape(1, ATTN_WIDTH)
    kg = jnp.tile(k_norm_g, N_KV_HEADS).reshape(1, KV_WIDTH)
    cog = conv_out_g.reshape(1, CONV_WIDTH)
    gmat = jnp.kron(jnp.eye(CONV_GROUPS, dtype=F32), jnp.ones((HEAD_DIM, HEAD_DIM), F32)).astype(BF16)
    cos_t, sin_t = _rope_tables(n)

    u_meta, k_meta, v_meta = pl.pallas_call(
        _meta_proj_kernel,
        out_shape=(jax.ShapeDtypeStruct((N_META, CONV_WIDTH), F32),
                   jax.ShapeDtypeStruct((N_KV_HEADS, N_META, HEAD_DIM), BF16),
                   jax.ShapeDtypeStruct((N_KV_HEADS, N_META, HEAD_DIM), BF16)),
        compiler_params=cparams(),
        name="meta_proj",
    )(meta_tokens, g_mix, w_in_b, kg, gmat[0:KV_WIDTH, 0:KV_WIDTH])
    pad = ((0, 0), (0, META_PAD - N_META), (0, 0))
    k_mp = jnp.pad(k_meta, pad)
    v_mp = jnp.concatenate([jnp.pad(v_meta, pad), jnp.ones((N_KV_HEADS, META_PAD, HEAD_DIM), BF16)], axis=-1)
    meta_bias = jnp.where(jnp.arange(META_PAD) < N_META, 0.0, NEG_BIG).astype(F32).reshape(1, META_PAD)

    const2 = lambda i, j: (0, 0)
    yconv, q, k, v = pl.pallas_call(
        functools.partial(_in_proj_kernel, n=n, rc=rc, bp=bp),
        grid=(b, n // rc),
        in_specs=[
            pl.BlockSpec((1, rc, D_MODEL), lambda i, j: (jnp.minimum(i, bp - 1), j, 0)),
            pl.BlockSpec((1, rc, D_MODEL), lambda i, j: (jnp.maximum(i - bp, 0), j, 0)),
            pl.BlockSpec((1, D_MODEL), const2),
            pl.BlockSpec((D_MODEL, IN_WIDTH), const2),
            pl.BlockSpec((3, CONV_WIDTH), const2),
            pl.BlockSpec((1, ATTN_WIDTH), const2),
            pl.BlockSpec((1, KV_WIDTH), const2),
            pl.BlockSpec((1, CONV_WIDTH), const2),
            pl.BlockSpec((rc, LANES), lambda i, j: (j, 0)),
            pl.BlockSpec((rc, LANES), lambda i, j: (j, 0)),
            pl.BlockSpec((1, CONV_WIDTH), const2),
            pl.BlockSpec((CONV_WIDTH, CONV_WIDTH), const2),
        ],
        out_specs=(
            pl.BlockSpec((1, n, CONV_WIDTH), lambda i, j: (i, 0, 0)),
            pl.BlockSpec((1, N_HEADS, rc, HEAD_DIM), lambda i, j: (i, 0, j, 0)),
            pl.BlockSpec((1, N_KV_HEADS, rc, HEAD_DIM), lambda i, j: (i, 0, j, 0)),
            pl.BlockSpec((1, N_KV_HEADS, rc, 2 * HEAD_DIM), lambda i, j: (i, 0, j, 0)),
        ),
        out_shape=(
            jax.ShapeDtypeStruct((b, n, CONV_WIDTH), BF16),
            jax.ShapeDtypeStruct((b, N_HEADS, n, HEAD_DIM), BF16),
            jax.ShapeDtypeStruct((b, N_KV_HEADS, n, HEAD_DIM), BF16),
            jax.ShapeDtypeStruct((b, N_KV_HEADS, n, 2 * HEAD_DIM), BF16),
        ),
        scratch_shapes=[pltpu.VMEM((n + 2 * SUBLANES, CONV_WIDTH), F32), pltpu.VMEM((n, CONV_WIDTH), F32)],
        compiler_params=cparams(dimension_semantics=("parallel", "arbitrary")),
        name="in_proj",
    )(x_prompt, x_sample, g_mix, w_in_b, conv_w, qg, kg, cog, cos_t, sin_t, u_meta[N_META - 1:N_META], gmat)

    h = pl.pallas_call(
        functools.partial(_attn_out_kernel, tq=tq, bp=bp, n=n, kc=min(1024, n)),
        grid=(b, n // tq),
        in_specs=[
            pl.BlockSpec((1, tq, D_MODEL), lambda i, j: (jnp.minimum(i, bp - 1), j, 0)),
            pl.BlockSpec((1, tq, D_MODEL), lambda i, j: (jnp.maximum(i - bp, 0), j, 0)),
            pl.BlockSpec((1, tq, CONV_WIDTH), lambda i, j: (i, j, 0)),
            pl.BlockSpec((1, N_HEADS, tq, HEAD_DIM), lambda i, j: (i, 0, j, 0)),
            pl.BlockSpec((1, N_KV_HEADS, n, HEAD_DIM), lambda i, j: (i, 0, 0, 0)),
            pl.BlockSpec((1, N_KV_HEADS, n, 2 * HEAD_DIM), lambda i, j: (i, 0, 0, 0)),
            pl.BlockSpec((N_KV_HEADS, META_PAD, HEAD_DIM), lambda i, j: (0, 0, 0)),
            pl.BlockSpec((N_KV_HEADS, META_PAD, 2 * HEAD_DIM), lambda i, j: (0, 0, 0)),
            pl.BlockSpec((1, META_PAD), const2),
            pl.BlockSpec((N_HEADS, HEAD_DIM), const2),
            pl.BlockSpec((CONV_WIDTH + ATTN_WIDTH, D_MODEL), const2),
        ],
        out_specs=pl.BlockSpec((1, tq, D_MODEL), lambda i, j: (i, j, 0)),
        out_shape=jax.ShapeDtypeStruct((b, n, D_MODEL), F32),
        scratch_shapes=[pltpu.VMEM((tq, ATTN_WIDTH), BF16)],
        compiler_params=cparams(dimension_semantics=("parallel", "parallel")),
        name="attn_out",
    )(x_prompt, x_sample, yconv, q, k, v, k_mp, v_mp, meta_bias, attn_out_g.reshape(N_HEADS, HEAD_DIM), w_out_b)

    wq_t = peer_wq.T.astype(BF16)
    sk = peer_subkeys.reshape(2 * PEER_HEADS, PEER_NKEYS, PEER_HALF).astype(BF16)
    u_b = peer_u.astype(BF16)
    v_t = peer_v.T.astype(BF16)
    nc = tt // LANES
    n_eb = PEER_EXPERTS // et
    sel_f32 = pltpu.VMEM((PEER_HEADS, nc, PEER_NKEYS, LANES), F32)
    sel_b16 = pltpu.VMEM((PEER_HEADS, nc, PEER_NKEYS // PACK, PACK, LANES), BF16)
    tp = bp * n // tt
    out_p, out_s = pl.pallas_call(
        functools.partial(_peer_kernel, tt=tt, et=et, tp=tp),
        grid=(t // tt, n_eb),
        in_specs=[
            pl.BlockSpec((tt, D_MODEL), lambda i, j: (i, 0)),
            pl.BlockSpec((1, D_MODEL), const2),
            pl.BlockSpec((PEER_HEADS * PEER_QDIM, D_MODEL), const2),
            pl.BlockSpec((2 * PEER_HEADS, PEER_NKEYS, PEER_HALF), lambda i, j: (0, 0, 0)),
            pl.BlockSpec((et, D_MODEL), lambda i, j: (j, 0)),
            pl.BlockSpec((D_MODEL, et // 2), lambda i, j: (0, 2 * j)),
            pl.BlockSpec((D_MODEL, et // 2), lambda i, j: (0, jnp.maximum(2 * j - 1, 0))),
            pl.BlockSpec((D_MODEL, et // 2), lambda i, j: (0, 2 * n_eb - 1)),
        ],
        out_specs=(pl.BlockSpec((tt, D_MODEL), lambda i, j: (jnp.minimum(i, tp - 1), 0)),
                   pl.BlockSpec((tt, D_MODEL), lambda i, j: (jnp.maximum(i - tp, 0), 0))),
        out_shape=(jax.ShapeDtypeStruct((bp * n, D_MODEL), F32),
                   jax.ShapeDtypeStruct((t - bp * n, D_MODEL), F32)),
        scratch_shapes=[
            pltpu.VMEM((D_MODEL, tt), BF16),
            pltpu.VMEM((2 * PEER_HEADS, nc, PEER_NKEYS, LANES), F32),
            sel_b16, sel_b16, sel_f32, sel_f32,
            pltpu.VMEM((D_MODEL, tt), F32),
            pltpu.VMEM((et // 2, tt), BF16),
            pltpu.VMEM((et // 2, tt), BF16),
        ],
        compiler_params=cparams(dimension_semantics=("arbitrary", "arbitrary")),
        name="peer",
    )(h.reshape(t, D_MODEL), norm_ffn_g.reshape(1, D_MODEL), wq_t, sk, u_b, v_t, v_t, v_t)
    return out_p.reshape(bp, n, D_MODEL), out_s.reshape(b - bp, n, D_MODEL)


def kernel(x_prompt, x_sample, meta_tokens, norm_mix_g, w_in, conv_w, q_norm_g, k_norm_g, conv_out_g,
           attn_out_g, w_out, norm_ffn_g, peer_wq, peer_subkeys, peer_u, peer_v):
    assert x_prompt.shape[1:] == x_sample.shape[1:]
    return _encode_all(x_prompt, x_sample, meta_tokens, norm_mix_g[0], w_in[0], conv_w[0], q_norm_g[0],
                       k_norm_g[0], conv_out_g[0], attn_out_g[0], w_out[0], norm_ffn_g[0], peer_wq[0],
                       peer_subkeys[0], peer_u[0], peer_v[0])
```

```python
import functools

import jax
import jax.numpy as jnp
from jax import lax
from jax.experimental import pallas as pl
from jax.experimental.pallas import tpu as pltpu

D_MODEL = 1024
N_META = 16
GRID_W = 64
CONV_WIDTH = 512
CONV_GROUPS = 8
N_HEADS = 8
N_KV_HEADS = 2
Q_PER_KV = N_HEADS // N_KV_HEADS
HEAD_DIM = 64
AXIS_DIM = HEAD_DIM // 2
ROPE_THETA = 10000.0
ATTN_WIDTH = N_HEADS * HEAD_DIM
KV_WIDTH = N_KV_HEADS * HEAD_DIM
IN_WIDTH = 3 * CONV_WIDTH + ATTN_WIDTH + 2 * KV_WIDTH
PEER_HEADS = 8
PEER_NKEYS = 128
PEER_EXPERTS = PEER_NKEYS * PEER_NKEYS
PEER_QDIM = 256
PEER_HALF = PEER_QDIM // 2
PEER_TOPK = 16
EPS = 1e-6

LANES = 128
SUBLANES = 8
META_PAD = LANES
NEG_BIG = -1e30
LOG2E = 1.4426950408889634
NOT_RANKED = 99.0
VMEM_LIMIT = 56 * 1024 * 1024

F32 = jnp.float32
BF16 = jnp.bfloat16


def _rms_rows(x, g):
    return x * lax.rsqrt(jnp.mean(x * x, axis=-1, keepdims=True) + EPS) * g


def _group_rms(x, gmat, g):
    sq = x * x
    hi = sq.astype(BF16)
    lo = (sq - hi.astype(F32)).astype(BF16)
    ss = jnp.dot(hi, gmat, preferred_element_type=F32) + jnp.dot(lo, gmat, preferred_element_type=F32)
    return x * lax.rsqrt(ss * (1.0 / HEAD_DIM) + EPS) * g


def _rope(x, cos, sin):
    lane = lax.broadcasted_iota(jnp.int32, x.shape, 1)
    swapped = jnp.where(lane % (2 * (AXIS_DIM // 2)) < AXIS_DIM // 2,
                        pltpu.roll(x, LANES - AXIS_DIM // 2, 1), pltpu.roll(x, AXIS_DIM // 2, 1))
    return x * cos + swapped * sin


def _meta_proj_kernel(m_ref, g_ref, w_ref, kg_ref, gm_ref, u_ref, k_ref, v_ref):
    xn = _rms_rows(m_ref[...], g_ref[...]).astype(BF16)
    z = jnp.dot(xn, w_ref[...], preferred_element_type=F32)
    u_ref[...] = z[:, CONV_WIDTH:2 * CONV_WIDTH] * z[:, 2 * CONV_WIDTH:3 * CONV_WIDTH]
    k0 = 3 * CONV_WIDTH + ATTN_WIDTH
    kn = _group_rms(z[:, k0:k0 + KV_WIDTH], gm_ref[...], kg_ref[...])
    for g in range(N_KV_HEADS):
        k_ref[g] = kn[:, g * HEAD_DIM:(g + 1) * HEAD_DIM].astype(BF16)
        v_ref[g] = z[:, k0 + KV_WIDTH + g * HEAD_DIM:k0 + KV_WIDTH + (g + 1) * HEAD_DIM].astype(BF16)


def _in_proj_kernel(xp_ref, xs_ref, g_ref, w_ref, cw_ref, qg_ref, kg_ref, cog_ref, cos_ref, sin_ref, um_ref,
                    gm_ref, yconv_ref, q_ref, k_ref, v_ref, u_scr, gb_scr, *, n, rc, bp):
    c = pl.program_id(1)
    r0 = pl.multiple_of(c * rc, rc)
    halo = SUBLANES

    @pl.when(c == 0)
    def _():
        u_scr[0:halo, :] = jnp.broadcast_to(um_ref[...], (halo, CONV_WIDTH))
        u_scr[halo + n:2 * halo + n, :] = jnp.zeros((halo, CONV_WIDTH), F32)

    x = jnp.where(pl.program_id(0) < bp, xp_ref[0], xs_ref[0])
    xn = _rms_rows(x, g_ref[...]).astype(BF16)
    z = jnp.dot(xn, w_ref[...], preferred_element_type=F32)
    gb_scr[pl.ds(r0, rc), :] = z[:, 0:CONV_WIDTH]
    u_scr[pl.ds(halo + r0, rc), :] = z[:, CONV_WIDTH:2 * CONV_WIDTH] * z[:, 2 * CONV_WIDTH:3 * CONV_WIDTH]

    cos = cos_ref[...]
    sin = sin_ref[...]
    q0 = 3 * CONV_WIDTH
    qn = _group_rms(z[:, q0:q0 + ATTN_WIDTH], gm_ref[...], qg_ref[...])
    for j in range(ATTN_WIDTH // LANES):
        qr = _rope(qn[:, j * LANES:(j + 1) * LANES], cos, sin) * (HEAD_DIM ** -0.5 * LOG2E)
        q_ref[0, 2 * j] = qr[:, :HEAD_DIM].astype(BF16)
        q_ref[0, 2 * j + 1] = qr[:, HEAD_DIM:].astype(BF16)
    k0 = q0 + ATTN_WIDTH
    kr = _rope(_group_rms(z[:, k0:k0 + KV_WIDTH], gm_ref[0:KV_WIDTH, 0:KV_WIDTH], kg_ref[...]), cos, sin)
    for g in range(N_KV_HEADS):
        k_ref[0, g] = kr[:, g * HEAD_DIM:(g + 1) * HEAD_DIM].astype(BF16)
        v_ref[0, g, :, 0:HEAD_DIM] = z[:, k0 + KV_WIDTH + g * HEAD_DIM:
                                       k0 + KV_WIDTH + (g + 1) * HEAD_DIM].astype(BF16)
        v_ref[0, g, :, HEAD_DIM:2 * HEAD_DIM] = jnp.ones((rc, HEAD_DIM), BF16)

    @pl.when(c == pl.num_programs(1) - 1)
    def _():
        for j in range(n // rc):
            s = j * rc
            um = u_scr[halo + s - 1:halo + s - 1 + rc, :]
            uc = u_scr[halo + s:halo + s + rc, :]
            up = u_scr[halo + s + 1:halo + s + 1 + rc, :]
            y = gb_scr[s:s + rc, :] * (um * cw_ref[0:1, :] + uc * cw_ref[1:2, :] + up * cw_ref[2:3, :])
            yconv_ref[0, s:s + rc, :] = _group_rms(y, gm_ref[...], cog_ref[...]).astype(BF16)


def _attn_out_kernel(xp_ref, xs_ref, yconv_ref, q_ref, k_ref, v_ref, km_ref, vm_ref, bias_ref, ag_ref, wo_ref,
                     h_ref, y_scr, *, tq, bp, n, kc):
    dn = (((1,), (1,)), ((), ()))
    rows = Q_PER_KV * tq
    qs = [q_ref[0, g * Q_PER_KV:(g + 1) * Q_PER_KV].reshape(rows, HEAD_DIM) for g in range(N_KV_HEADS)]
    state = []
    for g in range(N_KV_HEADS):
        sm = lax.dot_general(qs[g], km_ref[g], dn, preferred_element_type=F32) + bias_ref[...]
        m = jnp.max(sm, axis=-1, keepdims=True)
        p = jnp.exp2(sm - m).astype(BF16)
        state += [m, jnp.dot(p, vm_ref[g], preferred_element_type=F32)]

    def chunk(c, st):
        k0 = pl.multiple_of(c * kc, kc)
        out = []
        for g in range(N_KV_HEADS):
            m, acc = st[2 * g], st[2 * g + 1]
            s = lax.dot_general(qs[g], k_ref[0, g, pl.ds(k0, kc), :], dn, preferred_element_type=F32)
            m_new = jnp.maximum(m, jnp.max(s, axis=-1, keepdims=True))
            p = jnp.exp2(s - m_new).astype(BF16)
            pv = jnp.dot(p, v_ref[0, g, pl.ds(k0, kc), :], preferred_element_type=F32)
            out += [m_new, jnp.exp2(m - m_new) * acc + pv]
        return tuple(out)

    state = lax.fori_loop(0, n // kc, chunk, tuple(state))
    for g in range(N_KV_HEADS):
        acc = state[2 * g + 1]
        o = acc[:, 0:HEAD_DIM] / acc[:, HEAD_DIM:HEAD_DIM + 1]
        for r in range(Q_PER_KV):
            head = g * Q_PER_KV + r
            y = _rms_rows(o[r * tq:(r + 1) * tq], ag_ref[head:head + 1, :])
            y_scr[:, head * HEAD_DIM:(head + 1) * HEAD_DIM] = y.astype(BF16)
    x = jnp.where(pl.program_id(0) < bp, xp_ref[0], xs_ref[0])
    h_ref[0] = (x + jnp.dot(yconv_ref[0], wo_ref[0:CONV_WIDTH, :], preferred_element_type=F32)
                + jnp.dot(y_scr[...], wo_ref[CONV_WIDTH:CONV_WIDTH + ATTN_WIDTH, :],
                          preferred_element_type=F32))


PACK = 16
N_CAND = 80


def _top16_ranks(s):
    rows = lax.broadcasted_iota(jnp.int32, s.shape, 0)
    r16 = lax.broadcasted_iota(jnp.int32, (PEER_TOPK, LANES), 0)
    rank = jnp.full(s.shape, NOT_RANKED, F32)
    vals = jnp.zeros((PEER_TOPK, LANES), F32)
    for a in range(PEER_TOPK):
        m = jnp.max(s, axis=0, keepdims=True)
        idx = jnp.min(jnp.where(s == m, rows, PEER_NKEYS), axis=0, keepdims=True)
        sel = rows == idx
        rank = jnp.where(sel, float(a), rank)
        s = jnp.where(sel, -jnp.inf, s)
        vals = jnp.where(r16 == a, m, vals)
    return rank, vals


def _batcher_pairs(lo, hi):
    def merge(lo, hi, r):
        step = r * 2
        if step < hi - lo:
            yield from merge(lo, hi, step)
            yield from merge(lo + r, hi, step)
            yield from [(i, i + r) for i in range(lo + r, hi - r, step)]
        else:
            yield (lo, lo + r)
    if hi - lo >= 1:
        mid = lo + (hi - lo) // 2
        yield from _batcher_pairs(lo, mid)
        yield from _batcher_pairs(mid + 1, hi)
        yield from merge(lo, hi, 1)


def _compare_exchange(t, pairs):
    for i, j in pairs:
        t[i], t[j] = jnp.maximum(t[i], t[j]), jnp.minimum(t[i], t[j])


def _top16_values(s):
    k = PEER_TOPK
    t = [s[SUBLANES * i:SUBLANES * (i + 1)] for i in range(PEER_NKEYS // SUBLANES)]
    _compare_exchange(t, list(_batcher_pairs(0, k - 1)))
    bitonic = [(i, i + d) for d in (8, 4, 2, 1) for i in range(k) if not i & d]
    for shift in (1, 2, 4):
        t = [jnp.maximum(t[i], pltpu.roll(t[k - 1 - i], shift, 0)) for i in range(k)]
        _compare_exchange(t, bitonic)
    r16 = lax.broadcasted_iota(jnp.int32, (k, LANES), 0)
    vals = jnp.concatenate([t[0], t[0]], axis=0)
    for a in range(1, k):
        vals = jnp.where(r16 == a, jnp.concatenate([t[a], t[a]], axis=0), vals)
    return vals


def _pair_candidates(v1, v2):
    r8 = lax.broadcasted_iota(jnp.int32, (SUBLANES, LANES), 0)

    def bc(row):
        return jnp.broadcast_to(row, (SUBLANES, LANES))

    cands = [bc(v1[0:1]) + v2[0:8], bc(v1[0:1]) + v2[8:16]]
    poss = [r8, r8 + 8]
    for a in range(1, 8):
        cands.append(jnp.where(r8 < PEER_TOPK // (a + 1), bc(v1[a:a + 1]) + v2[0:8], -jnp.inf))
        poss.append(r8 + PEER_TOPK * a)
    cands.append(v1[8:16] + bc(v2[0:1]))
    poss.append((r8 + 8) * PEER_TOPK)
    return jnp.concatenate(cands, axis=0), jnp.concatenate(poss, axis=0)


def _pair_counts(cand0, sel):
    r8 = lax.broadcasted_iota(jnp.int32, (SUBLANES, LANES), 0)
    self32 = sel.astype(F32)
    z = jnp.sum(jnp.where(sel, jnp.exp(cand0 - cand0[0:1]), 0.0), axis=0, keepdims=True)
    n_lo = jnp.broadcast_to(jnp.sum(self32[0:16], axis=0, keepdims=True), (SUBLANES, LANES))
    for a in range(1, 8):
        na = jnp.sum(self32[8 + 8 * a:16 + 8 * a], axis=0, keepdims=True)
        n_lo = jnp.where(r8 == a, na, n_lo)
    return jnp.concatenate([n_lo, self32[N_CAND - 8:N_CAND]], axis=0), z


def _select_exact(s1, s2):
    rk1, v1 = _top16_ranks(s1)
    rk2, v2 = _top16_ranks(s2)
    cand0, pos = _pair_candidates(v1, v2)
    cand = cand0
    sel = jnp.zeros(cand.shape, jnp.bool_)
    for _ in range(PEER_TOPK):
        m = jnp.max(cand, axis=0, keepdims=True)
        first = jnp.min(jnp.where(cand == m, pos, PEER_TOPK * PEER_TOPK), axis=0, keepdims=True)
        hit = pos == first
        sel = jnp.logical_or(sel, hit)
        cand = jnp.where(hit, -jnp.inf, cand)
    counts, z = _pair_counts(cand0, sel)
    n_dense = jnp.zeros(s1.shape, F32)
    for a in range(PEER_TOPK):
        n_dense = jnp.where(rk1 == float(a), counts[a:a + 1], n_dense)
    return rk2, n_dense, z, v1[0:1], v2[0:1]


def _select_distinct(s1, s2):
    v1 = _top16_values(s1)
    v2 = _top16_values(s2)
    rk2 = jnp.zeros(s2.shape, F32)
    for b in range(PEER_TOPK):
        rk2 = jnp.where(s2 < v2[b:b + 1], float(b + 1), rk2)
    cand0, _ = _pair_candidates(v1, v2)
    m = cand0[0:1]
    for _ in range(1, PEER_TOPK):
        m = jnp.max(jnp.where(cand0 < m, cand0, -jnp.inf), axis=0, keepdims=True)
    sel = cand0 >= m
    counts, z = _pair_counts(cand0, sel)
    n_dense = jnp.zeros(s1.shape, F32)
    for a in range(PEER_TOPK):
        n_dense = jnp.where(s1 == v1[a:a + 1], counts[a:a + 1], n_dense)
    k = float(PEER_TOPK)
    n1 = jnp.sum((s1 >= v1[PEER_TOPK - 1:PEER_TOPK]).astype(F32), axis=0, keepdims=True)
    n2 = jnp.sum((rk2 < k).astype(F32), axis=0, keepdims=True)
    n3 = jnp.sum(sel.astype(F32), axis=0, keepdims=True)
    bad = jnp.logical_or(jnp.logical_or(n1 != k, n2 != k), n3 != k)
    dup = jnp.logical_or(v1[0:PEER_TOPK - 1] == v1[1:PEER_TOPK], v2[0:PEER_TOPK - 1] == v2[1:PEER_TOPK])
    return (rk2, n_dense, z, v1[0:1], v2[0:1]), jnp.logical_not(jnp.logical_or(jnp.any(bad), jnp.any(dup)))


def _bf16_pair_words(x):
    bits = lax.bitcast_convert_type(x.astype(BF16).astype(F32), jnp.uint32)
    return lax.bitcast_convert_type(bits | (bits >> 16), F32)


def _bf16_rows(word_row):
    return pltpu.bitcast(jnp.broadcast_to(word_row, (SUBLANES, LANES)), BF16)


def _peer_kernel(h_ref, g_ref, wqt_ref, sk_ref, u_ref, vta_ref, vtb_ref, vtl_ref, outp_ref, outs_ref,
                 xnt_scr, sc_scr, rk2_scr, e2_scr, n_scr, r_scr, acc_scr, pa_scr, pb_scr, *, tt, et, tp):
    j = pl.program_id(1)
    last = pl.num_programs(1) - 1
    nc = tt // LANES
    eh = et // 2
    n_ih = eh // PEER_NKEYS
    n_pk = PEER_NKEYS // PACK

    @pl.when(j == 0)
    def _prologue():
        xn = _rms_rows(h_ref[...], g_ref[...])
        xnt = xn.T.astype(BF16)
        xnt_scr[...] = xnt
        qt = jnp.dot(wqt_ref[...], xnt, preferred_element_type=F32).astype(BF16)
        for hp in range(2 * PEER_HEADS):
            st = jnp.dot(sk_ref[hp], qt[hp * PEER_HALF:(hp + 1) * PEER_HALF, :],
                         preferred_element_type=F32)
            for c in range(nc):
                sc_scr[hp, c] = st[:, c * LANES:(c + 1) * LANES]

        def head_body(hh, carry):
            def chunk_body(c, carry2):
                s1 = sc_scr[2 * hh, c]
                s2 = sc_scr[2 * hh + 1, c]

                def publish(rk2, n_dense, z, m1, m2):
                    rk2_scr[hh, c] = rk2.astype(BF16).reshape(n_pk, PACK, LANES)
                    e2_scr[hh, c] = jnp.exp(s2 - m2).astype(BF16).reshape(n_pk, PACK, LANES)
                    n_scr[hh, c] = _bf16_pair_words(n_dense)
                    r_scr[hh, c] = _bf16_pair_words(jnp.where(n_dense > 0.0, jnp.exp(s1 - m1) / z, 0.0))

                picked, distinct = _select_distinct(s1, s2)
                publish(*picked)

                @pl.when(jnp.logical_not(distinct))
                def _():
                    publish(*_select_exact(s1, s2))
                return carry2
            return lax.fori_loop(0, nc, chunk_body, carry)
        lax.fori_loop(0, PEER_HEADS, head_body, 0)
        acc_scr[...] = jnp.zeros(acc_scr.shape, F32)
        pb_scr[...] = jnp.zeros((eh, tt), BF16)

    def build(half, p_ref):
        a_t = jnp.dot(u_ref[half * eh:(half + 1) * eh, :], xnt_scr[...], preferred_element_type=F32)
        a_b = a_t.astype(BF16)
        hid = (0.5 * a_b) * (1.0 + lax.erf(a_b * (2.0 ** -0.5)))
        for ii in range(n_ih):
            i = (2 * j + half) * n_ih + ii
            for c in range(nc):
                w = jnp.zeros((n_pk, PACK, LANES), BF16)
                for hh in range(PEER_HEADS):
                    n_row = _bf16_rows(n_scr[hh, c, pl.ds(i, 1), :])
                    r_row = _bf16_rows(r_scr[hh, c, pl.ds(i, 1), :])
                    w = w + jnp.where(rk2_scr[hh, c] < n_row, e2_scr[hh, c], jnp.zeros((), BF16)) * r_row
                rows = slice(ii * PEER_NKEYS, (ii + 1) * PEER_NKEYS)
                cols = slice(c * LANES, (c + 1) * LANES)
                p_ref[rows, cols] = hid[rows, cols] * w.reshape(PEER_NKEYS, LANES)

    build(0, pa_scr)
    acc_scr[...] += jnp.dot(vtb_ref[...], pb_scr[...], preferred_element_type=F32)
    build(1, pb_scr)
    acc_scr[...] += jnp.dot(vta_ref[...], pa_scr[...], preferred_element_type=F32)

    @pl.when(j == last)
    def _epilogue():
        tail = jnp.dot(vtl_ref[...], pb_scr[...], preferred_element_type=F32)
        res = h_ref[...] + (acc_scr[...] + tail).T
        i = pl.program_id(0)

        @pl.when(i < tp)
        def _():
            outp_ref[...] = res

        @pl.when(i >= tp)
        def _():
            outs_ref[...] = res


def _rope_tables(n):
    rows = n // GRID_W
    row = jnp.repeat(jnp.arange(rows, dtype=jnp.int32), GRID_W).astype(F32)
    col = jnp.tile(jnp.arange(GRID_W, dtype=jnp.int32), rows).astype(F32)
    freqs = ROPE_THETA ** (-jnp.arange(0, AXIS_DIM, 2, dtype=F32) / AXIS_DIM)
    ang_r = row[:, None] * freqs[None, :]
    ang_c = col[:, None] * freqs[None, :]
    cos_h = jnp.concatenate([jnp.cos(ang_r)] * 2 + [jnp.cos(ang_c)] * 2, axis=-1)
    sin_h = jnp.concatenate([-jnp.sin(ang_r), jnp.sin(ang_r), -jnp.sin(ang_c), jnp.sin(ang_c)], axis=-1)
    return jnp.tile(cos_h, (1, LANES // HEAD_DIM)), jnp.tile(sin_h, (1, LANES // HEAD_DIM))


def _tile_sizes(n, t_prompt, t_sample):
    rc = min(512, n)
    tq = min(256, n)
    tt = min(512, t_prompt, t_sample)
    et = 1024
    assert n % rc == 0 and n % tq == 0 and n % GRID_W == 0 and tt % LANES == 0
    assert t_prompt % tt == 0 and t_sample % tt == 0
    return rc, tq, tt, et


def _encode_all(x_prompt, x_sample, meta_tokens, norm_mix_g, w_in, conv_w, q_norm_g, k_norm_g, conv_out_g, attn_out_g,
                w_out, norm_ffn_g, peer_wq, peer_subkeys, peer_u, peer_v):
    bp, n, _ = x_prompt.shape
    b = bp + x_sample.shape[0]
    t = b * n
    rc, tq, tt, et = _tile_sizes(n, bp * n, t - bp * n)
    cparams = functools.partial(pltpu.CompilerParams, vmem_limit_bytes=VMEM_LIMIT)

    w_in_b = w_in.astype(BF16)
    w_out_b = w_out.astype(BF16)
    g_mix = norm_mix_g.reshape(1, D_MODEL)
    qg = jnp.tile(q_norm_g, N_HEADS).reshape(1, ATTN_WIDTH)
    kg = jnp.tile(k_norm_g, N_KV_HEADS).reshape(1, KV_WIDTH)
    cog = conv_out_g.reshape(1, CONV_WIDTH)
    gmat = jnp.kron(jnp.eye(CONV_GROUPS, dtype=F32), jnp.ones((HEAD_DIM, HEAD_DIM), F32)).astype(BF16)
    cos_t, sin_t = _rope_tables(n)

    u_meta, k_meta, v_meta = pl.pallas_call(
        _meta_proj_kernel,
        out_shape=(jax.ShapeDtypeStruct((N_META, CONV_WIDTH), F32),
                   jax.ShapeDtypeStruct((N_KV_HEADS, N_META, HEAD_DIM), BF16),
                   jax.ShapeDtypeStruct((N_KV_HEADS, N_META, HEAD_DIM), BF16)),
        compiler_params=cparams(),
        name="meta_proj",
    )(meta_tokens, g_mix, w_in_b, kg, gmat[0:KV_WIDTH, 0:KV_WIDTH])
    pad = ((0, 0), (0, META_PAD - N_META), (0, 0))
    k_mp = jnp.pad(k_meta, pad)
    v_mp = jnp.concatenate([jnp.pad(v_meta, pad), jnp.ones((N_KV_HEADS, META_PAD, HEAD_DIM), BF16)], axis=-1)
    meta_bias = jnp.where(jnp.arange(META_PAD) < N_META, 0.0, NEG_BIG).astype(F32).reshape(1, META_PAD)

    const2 = lambda i, j: (0, 0)
    yconv, q, k, v = pl.pallas_call(
        functools.partial(_in_proj_kernel, n=n, rc=rc, bp=bp),
        grid=(b, n // rc),
        in_specs=[
            pl.BlockSpec((1, rc, D_MODEL), lambda i, j: (jnp.minimum(i, bp - 1), j, 0)),
            pl.BlockSpec((1, rc, D_MODEL), lambda i, j: (jnp.maximum(i - bp, 0), j, 0)),
            pl.BlockSpec((1, D_MODEL), const2),
            pl.BlockSpec((D_MODEL, IN_WIDTH), const2),
            pl.BlockSpec((3, CONV_WIDTH), const2),
            pl.BlockSpec((1, ATTN_WIDTH), const2),
            pl.BlockSpec((1, KV_WIDTH), const2),
            pl.BlockSpec((1, CONV_WIDTH), const2),
            pl.BlockSpec((rc, LANES), lambda i, j: (j, 0)),
            pl.BlockSpec((rc, LANES), lambda i, j: (j, 0)),
            pl.BlockSpec((1, CONV_WIDTH), const2),
            pl.BlockSpec((CONV_WIDTH, CONV_WIDTH), const2),
        ],
        out_specs=(
            pl.BlockSpec((1, n, CONV_WIDTH), lambda i, j: (i, 0, 0)),
            pl.BlockSpec((1, N_HEADS, rc, HEAD_DIM), lambda i, j: (i, 0, j, 0)),
            pl.BlockSpec((1, N_KV_HEADS, rc, HEAD_DIM), lambda i, j: (i, 0, j, 0)),
            pl.BlockSpec((1, N_KV_HEADS, rc, 2 * HEAD_DIM), lambda i, j: (i, 0, j, 0)),
        ),
        out_shape=(
            jax.ShapeDtypeStruct((b, n, CONV_WIDTH), BF16),
            jax.ShapeDtypeStruct((b, N_HEADS, n, HEAD_DIM), BF16),
            jax.ShapeDtypeStruct((b, N_KV_HEADS, n, HEAD_DIM), BF16),
            jax.ShapeDtypeStruct((b, N_KV_HEADS, n, 2 * HEAD_DIM), BF16),
        ),
        scratch_shapes=[pltpu.VMEM((n + 2 * SUBLANES, CONV_WIDTH), F32), pltpu.VMEM((n, CONV_WIDTH), F32)],
        compiler_params=cparams(dimension_semantics=("parallel", "arbitrary")),
        name="in_proj",
    )(x_prompt, x_sample, g_mix, w_in_b, conv_w, qg, kg, cog, cos_t, sin_t, u_meta[N_META - 1:N_META], gmat)

    h = pl.pallas_call(
        functools.partial(_attn_out_kernel, tq=tq, bp=bp, n=n, kc=min(1024, n)),
        grid=(b, n // tq),
        in_specs=[
            pl.BlockSpec((1, tq, D_MODEL), lambda i, j: (jnp.minimum(i, bp - 1), j, 0)),
            pl.BlockSpec((1, tq, D_MODEL), lambda i, j: (jnp.maximum(i - bp, 0), j, 0)),
            pl.BlockSpec((1, tq, CONV_WIDTH), lambda i, j: (i, j, 0)),
            pl.BlockSpec((1, N_HEADS, tq, HEAD_DIM), lambda i, j: (i, 0, j, 0)),
            pl.BlockSpec((1, N_KV_HEADS, n, HEAD_DIM), lambda i, j: (i, 0, 0, 0)),
            pl.BlockSpec((1, N_KV_HEADS, n, 2 * HEAD_DIM), lambda i, j: (i, 0, 0, 0)),
            pl.BlockSpec((N_KV_HEADS, META_PAD, HEAD_DIM), lambda i, j: (0, 0, 0)),
            pl.BlockSpec((N_KV_HEADS, META_PAD, 2 * HEAD_DIM), lambda i, j: (0, 0, 0)),
            pl.BlockSpec((1, META_PAD), const2),
            pl.BlockSpec((N_HEADS, HEAD_DIM), const2),
            pl.BlockSpec((CONV_WIDTH + ATTN_WIDTH, D_MODEL), const2),
        ],
        out_specs=pl.BlockSpec((1, tq, D_MODEL), lambda i, j: (i, j, 0)),
        out_shape=jax.ShapeDtypeStruct((b, n, D_MODEL), F32),
        scratch_shapes=[pltpu.VMEM((tq, ATTN_WIDTH), BF16)],
        compiler_params=cparams(dimension_semantics=("parallel", "parallel")),
        name="attn_out",
    )(x_prompt, x_sample, yconv, q, k, v, k_mp, v_mp, meta_bias, attn_out_g.reshape(N_HEADS, HEAD_DIM), w_out_b)

    wq_t = peer_wq.T.astype(BF16)
    sk = peer_subkeys.reshape(2 * PEER_HEADS, PEER_NKEYS, PEER_HALF).astype(BF16)
    u_b = peer_u.astype(BF16)
    v_t = peer_v.T.astype(BF16)
    nc = tt // LANES
    n_eb = PEER_EXPERTS // et
    sel_f32 = pltpu.VMEM((PEER_HEADS, nc, PEER_NKEYS, LANES), F32)
    sel_b16 = pltpu.VMEM((PEER_HEADS, nc, PEER_NKEYS // PACK, PACK, LANES), BF16)
    tp = bp * n // tt
    out_p, out_s = pl.pallas_call(
        functools.partial(_peer_kernel, tt=tt, et=et, tp=tp),
        grid=(t // tt, n_eb),
        in_specs=[
            pl.BlockSpec((tt, D_MODEL), lambda i, j: (i, 0)),
            pl.BlockSpec((1, D_MODEL), const2),
            pl.BlockSpec((PEER_HEADS * PEER_QDIM, D_MODEL), const2),
            pl.BlockSpec((2 * PEER_HEADS, PEER_NKEYS, PEER_HALF), lambda i, j: (0, 0, 0)),
            pl.BlockSpec((et, D_MODEL), lambda i, j: (j, 0)),
            pl.BlockSpec((D_MODEL, et // 2), lambda i, j: (0, 2 * j)),
            pl.BlockSpec((D_MODEL, et // 2), lambda i, j: (0, jnp.maximum(2 * j - 1, 0))),
            pl.BlockSpec((D_MODEL, et // 2), lambda i, j: (0, 2 * n_eb - 1)),
        ],
        out_specs=(pl.BlockSpec((tt, D_MODEL), lambda i, j: (jnp.minimum(i, tp - 1), 0)),
                   pl.BlockSpec((tt, D_MODEL), lambda i, j: (jnp.maximum(i - tp, 0), 0))),
        out_shape=(jax.ShapeDtypeStruct((bp * n, D_MODEL), F32),
                   jax.ShapeDtypeStruct((t - bp * n, D_MODEL), F32)),
        scratch_shapes=[
            pltpu.VMEM((D_MODEL, tt), BF16),
            pltpu.VMEM((2 * PEER_HEADS, nc, PEER_NKEYS, LANES), F32),
            sel_b16, sel_b16, sel_f32, sel_f32,
            pltpu.VMEM((D_MODEL, tt), F32),
            pltpu.VMEM((et // 2, tt), BF16),
            pltpu.VMEM((et // 2, tt), BF16),
        ],
        compiler_params=cparams(dimension_semantics=("arbitrary", "arbitrary")),
        name="peer",
    )(h.reshape(t, D_MODEL), norm_ffn_g.reshape(1, D_MODEL), wq_t, sk, u_b, v_t, v_t, v_t)
    return out_p.reshape(bp, n, D_MODEL), out_s.reshape(b - bp, n, D_MODEL)


def kernel(x_prompt, x_sample, meta_tokens, norm_mix_g, w_in, conv_w, q_norm_g, k_norm_g, conv_out_g,
           attn_out_g, w_out, norm_ffn_g, peer_wq, peer_subkeys, peer_u, peer_v):
    assert x_prompt.shape[1:] == x_sample.shape[1:]
    return _encode_all(x_prompt, x_sample, meta_tokens, norm_mix_g[0], w_in[0], conv_w[0], q_norm_g[0],
                       k_norm_g[0], conv_out_g[0], attn_out_g[0], w_out[0], norm_ffn_g[0], peer_wq[0],
                       peer_subkeys[0], peer_u[0], peer_v[0])
```
